```python
import math
import jax, jax.numpy as jnp
from jax import lax
import numpy as np

D_MODEL = 1024
BATCH = 8
SEQ = 4096
DEPTH = 2

HEAD_DIM = 64
N_MEM = 256
MEM_HEADS = 4
MEM_WIDTH = MEM_HEADS * HEAD_DIM
RWKV_HEADS = 12
RWKV_WIDTH = RWKV_HEADS * HEAD_DIM
DECAY_LORA = 64
AAA_LORA = 64
GATE_LORA = 128
SHIFT_WIDTH = 3 * RWKV_WIDTH + DECAY_LORA + AAA_LORA + GATE_LORA
A_IN_WIDTH = SHIFT_WIDTH + MEM_WIDTH
A_OUT_WIDTH = RWKV_WIDTH + MEM_WIDTH
DIL_GROUPS = ((128, 1), (512, 4), (2048, 16))
DIL_HEADS_PER_GROUP = 4
DIL_HEADS = DIL_HEADS_PER_GROUP * len(DIL_GROUPS)
DIL_WIDTH = DIL_HEADS * HEAD_DIM
DIL_BLOCK = 128
B_IN_WIDTH = DIL_WIDTH + MEM_WIDTH
B_OUT_WIDTH = DIL_HEADS_PER_GROUP * HEAD_DIM + MEM_WIDTH
D_FF = 2816
CONV_WIDTH = 3
ROPE_THETA = 10000.0
RMS_EPS = 1e-6
LNX_EPS = 64e-5
NEG_INF = -1e30
N_A = DEPTH // 2
N_B = DEPTH - N_A

kernel_name = "yoco_rwkv7_dilated_memory_convffn"


def f32(a):
    return a.astype(jnp.float32)


def rmsnorm(x, g):
    xf = f32(x)
    return xf * lax.rsqrt(jnp.mean(xf * xf, axis=-1, keepdims=True) + RMS_EPS) * f32(g)


def rope_tables(T):
    inv = ROPE_THETA ** (-jnp.arange(0, HEAD_DIM, 2, dtype=jnp.float32) / HEAD_DIM)
    ang = jnp.arange(T, dtype=jnp.float32)[:, None] * inv[None, :]
    return jnp.cos(ang)[:, None, :], jnp.sin(ang)[:, None, :]


def apply_rope(z, cos, sin):
    half = HEAD_DIM // 2
    z1, z2 = z[..., :half], z[..., half:]
    return jnp.concatenate([z1 * cos - z2 * sin, z2 * cos + z1 * sin], axis=-1)


def rwkv7_time_mix(p, mu, w0, w2, a0, a2, g2, k_k, k_a, r_k, lnx_w, lnx_b):
    B, T, _ = p.shape
    prev = jnp.pad(p[:, :-1], ((0, 0), (1, 0), (0, 0)))
    xs = p + (prev - p) * f32(mu)
    cuts = np.cumsum([RWKV_WIDTH, RWKV_WIDTH, RWKV_WIDTH, DECAY_LORA, AAA_LORA]).tolist()
    r, k, v, wd, ad, gd = jnp.split(xs, cuts, axis=-1)
    w_log = -jax.nn.softplus(-(f32(w0) + jnp.tanh(wd) @ f32(w2))) - 0.5
    decay = jnp.exp(-jnp.exp(w_log))
    a = jax.nn.sigmoid(f32(a0) + ad @ f32(a2))
    g = jax.nn.sigmoid(gd) @ f32(g2)
    heads = lambda z: z.reshape(B, T, RWKV_HEADS, HEAD_DIM)
    kk = heads(k * f32(k_k))
    kk = kk / jnp.maximum(jnp.linalg.norm(kk, axis=-1, keepdims=True), 1e-12)
    k = k * (1.0 + (a - 1.0) * f32(k_a))
    r_h, k_h, v_h, w_h, a_h = heads(r), heads(k), heads(v), heads(decay), heads(a)

    def step(S, inp):
        r_t, w_t, k_t, v_t, kk_t, a_t = inp
        sa = jnp.einsum('bhvk,bhk->bhv', S, -kk_t)
        S = (S * w_t[:, :, None, :] + sa[..., None] * (kk_t * a_t)[:, :, None, :]
             + v_t[..., None] * k_t[:, :, None, :])
        return S, jnp.einsum('bhvk,bhk->bhv', S, r_t)

    seq = tuple(jnp.moveaxis(z, 1, 0) for z in (r_h, w_h, k_h, v_h, kk, a_h))
    S0 = jnp.zeros((B, RWKV_HEADS, HEAD_DIM, HEAD_DIM), jnp.float32)
    _, ys = lax.scan(step, S0, seq)
    y = jnp.moveaxis(ys, 0, 1)
    m = jnp.mean(y, axis=-1, keepdims=True)
    var = jnp.mean((y - m) ** 2, axis=-1, keepdims=True)
    y = ((y - m) * lax.rsqrt(var + LNX_EPS)).reshape(B, T, RWKV_WIDTH) * f32(lnx_w) + f32(lnx_b)
    bonus = (jnp.sum(r_h * k_h * f32(r_k), axis=-1, keepdims=True) * v_h).reshape(B, T, RWKV_WIDTH)
    return (y + bonus) * g


def dilated_group_attention(q, k, v, window, dilation):
    B, T, H, Dh = q.shape
    back = window // dilation
    span = dilation * DIL_BLOCK
    Tp = ((T + span - 1) // span) * span
    U = Tp // dilation
    nb = U // DIL_BLOCK

    def to_blocks(z):
        z = jnp.pad(z, ((0, 0), (0, Tp - T), (0, 0), (0, 0)))
        z = z.reshape(B, U, dilation, H, Dh).transpose(0, 3, 2, 1, 4)
        return z.reshape(B, H, dilation, nb, DIL_BLOCK, Dh)

    def with_prev(z):
        prev = jnp.pad(z, ((0, 0), (0, 0), (0, 0), (1, 0), (0, 0), (0, 0)))[:, :, :, :-1]
        return jnp.concatenate([prev, z], axis=4)

    qb = to_blocks(q)
    kw, vw = with_prev(to_blocks(k)), with_prev(to_blocks(v))
    s = jnp.einsum('bhrnid,bhrnjd->bhrnij', qb, kw) / math.sqrt(Dh)
    i = jnp.arange(DIL_BLOCK)[:, None]
    jj = jnp.arange(2 * DIL_BLOCK)[None, :]
    dist = DIL_BLOCK + i - jj
    band = (dist >= 0) & (dist <= back)
    valid = band[None] & ((jnp.arange(nb) > 0)[:, None, None] | (jj >= DIL_BLOCK)[None])
    s = jnp.where(valid, s, NEG_INF)
    mx = jnp.max(s, axis=-1, keepdims=True)
    pr = jnp.exp(s - mx)
    den = jnp.sum(pr, axis=-1, keepdims=True)
    o = jnp.einsum('bhrnij,bhrnjd->bhrnid', pr, vw) / den
    lse = (mx + jnp.log(den))[..., 0]
    o = o.reshape(B, H, dilation, U, Dh).transpose(0, 3, 2, 1, 4).reshape(B, Tp, H, Dh)[:, :T]
    lse = lse.reshape(B, H, dilation, U).transpose(0, 3, 2, 1).reshape(B, Tp, H)[:, :T]
    return o, lse


def dilated_attention(q, k, v):
    B, T = q.shape[:2]
    outs, lses = [], []
    for gi, (win, dil) in enumerate(DIL_GROUPS):
        hs = slice(gi * DIL_HEADS_PER_GROUP, (gi + 1) * DIL_HEADS_PER_GROUP)
        o, l = dilated_group_attention(q[:, :, hs], k[:, :, hs], v[:, :, hs], win, dil)
        outs.append(o)
        lses.append(l)
    wgt = jax.nn.softmax(jnp.stack(lses, axis=0), axis=0)
    o = jnp.sum(wgt[..., None] * jnp.stack(outs, axis=0), axis=0)
    return o.reshape(B, T, DIL_HEADS_PER_GROUP * HEAD_DIM)


def memory_attention(q, mem, norm_g, w_kv, q_norm, k_norm):
    B, T, _ = q.shape
    M = mem.shape[1]
    kv = f32(rmsnorm(mem, norm_g).astype(mem.dtype) @ w_kv)
    k = rmsnorm(kv[..., :MEM_WIDTH].reshape(B, M, MEM_HEADS, HEAD_DIM), k_norm)
    v = kv[..., MEM_WIDTH:].reshape(B, M, MEM_HEADS, HEAD_DIM)
    qh = rmsnorm(q.reshape(B, T, MEM_HEADS, HEAD_DIM), q_norm)
    s = jnp.einsum('bthd,bmhd->bhtm', qh, k) / math.sqrt(HEAD_DIM)
    p = jax.nn.softmax(s, axis=-1)
    return jnp.einsum('bhtm,bmhd->bthd', p, v).reshape(B, T, MEM_WIDTH)


def conv_ffn(x, g, w_up, conv_w, conv_b, w_down):
    T = x.shape[1]
    u = f32(rmsnorm(x, g).astype(x.dtype) @ w_up)
    up = jnp.pad(u, ((0, 0), (CONV_WIDTH - 1, 0), (0, 0)))
    c = f32(conv_b) + sum(f32(conv_w[j]) * up[:, j:j + T] for j in range(CONV_WIDTH))
    gate, val = jnp.split(c, 2, axis=-1)
    z = jax.nn.silu(gate) * val
    return z.astype(x.dtype) @ w_down


def setup_inputs(seed: int = 0) -> dict:
    key = jax.random.key(seed)
    ks = iter(jax.random.split(key, 40))
    D = D_MODEL

    def nrm(shape, scale):
        return scale * jax.random.normal(next(ks), shape, jnp.float32)

    def unif(shape, lo, hi):
        return jax.random.uniform(next(ks), shape, jnp.float32, lo, hi)

    def gain(shape):
        return 1.0 + nrm(shape, 0.02)

    return {
        "x": nrm((BATCH, SEQ, D), 1.0),
        "mem": nrm((BATCH, N_MEM, D), 1.0),
        "attn_norm": gain((DEPTH, D)),
        "a_w_in": nrm((N_A, D, A_IN_WIDTH), D ** -0.5),
        "a_mu": unif((N_A, SHIFT_WIDTH), 0.0, 1.0),
        "a_w0": unif((N_A, RWKV_WIDTH), -5.0, 0.0),
        "a_w2": nrm((N_A, DECAY_LORA, RWKV_WIDTH), 0.5 * DECAY_LORA ** -0.5),
        "a_a0": nrm((N_A, RWKV_WIDTH), 0.5),
        "a_a2": nrm((N_A, AAA_LORA, RWKV_WIDTH), AAA_LORA ** -0.5),
        "a_g2": nrm((N_A, GATE_LORA, RWKV_WIDTH), GATE_LORA ** -0.5),
        "a_k_k": 0.85 + nrm((N_A, RWKV_WIDTH), 0.05),
        "a_k_a": 1.0 + nrm((N_A, RWKV_WIDTH), 0.05),
        "a_r_k": nrm((N_A, RWKV_HEADS, HEAD_DIM), 0.1),
        "a_lnx_w": gain((N_A, RWKV_WIDTH)),
        "a_lnx_b": nrm((N_A, RWKV_WIDTH), 0.02),
        "a_w_out": nrm((N_A, A_OUT_WIDTH, D), A_OUT_WIDTH ** -0.5),
        "kv_norm": gain((D,)),
        "kv_w": nrm((D, 2 * DIL_WIDTH), D ** -0.5),
        "kv_k_norm": gain((HEAD_DIM,)),
        "b_w_in": nrm((N_B, D, B_IN_WIDTH), D ** -0.5),
        "b_q_norm": gain((N_B, HEAD_DIM)),
        "b_w_out": nrm((N_B, B_OUT_WIDTH, D), B_OUT_WIDTH ** -0.5),
        "mem_norm": gain((DEPTH, D)),
        "mem_w_kv": nrm((DEPTH, D, 2 * MEM_WIDTH), D ** -0.5),
        "mem_q_norm": gain((DEPTH, HEAD_DIM)),
        "mem_k_norm": gain((DEPTH, HEAD_DIM)),
        "ffn_norm": gain((DEPTH, D)),
        "ffn_w_up": nrm((DEPTH, D, 2 * D_FF), D ** -0.5),
        "ffn_conv_w": nrm((DEPTH, CONV_WIDTH, 2 * D_FF), CONV_WIDTH ** -0.5),
        "ffn_conv_b": nrm((DEPTH, 2 * D_FF), 0.02),
        "ffn_w_down": nrm((DEPTH, D_FF, D), D_FF ** -0.5),
    }


def reference(x, mem, attn_norm, a_w_in, a_mu, a_w0, a_w2, a_a0, a_a2, a_g2, a_k_k, a_k_a,
              a_r_k, a_lnx_w, a_lnx_b, a_w_out, kv_norm, kv_w, kv_k_norm, b_w_in, b_q_norm,
              b_w_out, mem_norm, mem_w_kv, mem_q_norm, mem_k_norm, ffn_norm, ffn_w_up,
              ffn_conv_w, ffn_conv_b, ffn_w_down):
    B, T, _ = x.shape
    cos, sin = rope_tables(T)
    k_sh = v_sh = None
    for i in range(DEPTH):
        h = rmsnorm(x, attn_norm[i]).astype(x.dtype)
        if i < N_A:
            j = i
            p = f32(h @ a_w_in[j])
            y_mix = rwkv7_time_mix(p[..., :SHIFT_WIDTH], a_mu[j], a_w0[j], a_w2[j], a_a0[j],
                                   a_a2[j], a_g2[j], a_k_k[j], a_k_a[j], a_r_k[j],
                                   a_lnx_w[j], a_lnx_b[j])
            q_mem = p[..., SHIFT_WIDTH:]
            w_out = a_w_out[j]
        else:
            j = i - N_A
            if j == 0:
                kvp = f32(rmsnorm(x, kv_norm).astype(x.dtype) @ kv_w)
                k_sh = apply_rope(rmsnorm(kvp[..., :DIL_WIDTH].reshape(B, T, DIL_HEADS, HEAD_DIM),
                                          kv_k_norm), cos, sin)
                v_sh = kvp[..., DIL_WIDTH:].reshape(B, T, DIL_HEADS, HEAD_DIM)
            p = f32(h @ b_w_in[j])
            q = apply_rope(rmsnorm(p[..., :DIL_WIDTH].reshape(B, T, DIL_HEADS, HEAD_DIM),
                                   b_q_norm[j]), cos, sin)
            y_mix = dilated_attention(q, k_sh, v_sh)
            q_mem = p[..., DIL_WIDTH:]
            w_out = b_w_out[j]
        y_mem = memory_attention(q_mem, mem, mem_norm[i], mem_w_kv[i], mem_q_norm[i], mem_k_norm[i])
        y = jnp.concatenate([y_mix, y_mem], axis=-1).astype(x.dtype)
        x = x + y @ w_out
        x = x + conv_ffn(x, ffn_norm[i], ffn_w_up[i], ffn_conv_w[i], ffn_conv_b[i], ffn_w_down[i])
    return x
```

```python
import functools
import math

import jax
import jax.numpy as jnp
from jax import lax
from jax.experimental import pallas as pl
from jax.experimental.pallas import tpu as pltpu

F32 = jnp.float32
BF16 = jnp.bfloat16

D_MODEL = 1024
HEAD_DIM = 64
N_MEM = 256
MEM_HEADS = 4
MEM_WIDTH = MEM_HEADS * HEAD_DIM
RWKV_HEADS = 12
RWKV_WIDTH = RWKV_HEADS * HEAD_DIM
DECAY_LORA = 64
AAA_LORA = 64
GATE_LORA = 128
SHIFT_WIDTH = 3 * RWKV_WIDTH + DECAY_LORA + AAA_LORA + GATE_LORA
A_IN_WIDTH = SHIFT_WIDTH + MEM_WIDTH
DIL_GROUPS = ((128, 1), (512, 4), (2048, 16))
DIL_HEADS_PER_GROUP = 4
GROUP_WIDTH = DIL_HEADS_PER_GROUP * HEAD_DIM
DIL_WIDTH = GROUP_WIDTH * len(DIL_GROUPS)
DIL_BLOCK = 128
D_FF = 2816
ROPE_THETA = 10000.0
RMS_EPS = 1e-6
LNX_EPS = 64e-5
NEG_INF = -1e30

LANES = 128
PAIR = 2 * HEAD_DIM
CHUNK = 64
VMEM_LIMIT = 52 * 1024 * 1024

NN = ((1,), (0,))
NT = ((1,), (1,))
TN = ((0,), (0,))


def _dg(a, b, dims=NN):
    return lax.dot_general(a, b, (dims, ((), ())), preferred_element_type=F32)


def _bdot(a, b, dims=NN):
    return _dg(a.astype(BF16), b.astype(BF16), dims)


def _split2(x):
    hi = x.astype(BF16)
    lo = (x - hi.astype(F32)).astype(BF16)
    return hi, lo


def _dot3(a, b, dims=NN):
    ah, al = _split2(a)
    bh, bl = _split2(b)
    return _dg(ah, bh, dims) + _dg(ah, bl, dims) + _dg(al, bh, dims)


def _seg_sum(x, ones_bd):
    w = ones_bd.shape[0]
    outs = []
    for j in range(x.shape[1] // w):
        hi, lo = _split2(x[:, j * w:(j + 1) * w])
        outs.append(_dg(hi, ones_bd) + _dg(lo, ones_bd))
    return outs[0] if len(outs) == 1 else jnp.concatenate(outs, axis=1)


def _rms_scale(x):
    return lax.rsqrt(jnp.mean(x * x, axis=-1, keepdims=True) + RMS_EPS)


def _sigmoid(z):
    return 1.0 / (1.0 + jnp.exp(-z))


def _head_rms(z, gain_tiled, ones_bd):
    ms = _seg_sum(z * z, ones_bd) * (1.0 / HEAD_DIM)
    return z * lax.rsqrt(ms + RMS_EPS) * gain_tiled


def _mem_attn(qm, kmem, vmem, qn_tiled, ones_bd):
    qh = _head_rms(qm, qn_tiled, ones_bd) * (1.0 / math.sqrt(HEAD_DIM))
    lane = lax.broadcasted_iota(jnp.int32, (1, MEM_WIDTH), 1)
    out = jnp.zeros_like(qm)
    for h in range(MEM_HEADS):
        mask = (lane >= h * HEAD_DIM) & (lane < (h + 1) * HEAD_DIM)
        s = _dg(jnp.where(mask, qh, 0.0).astype(BF16), kmem, NT)
        mx = jnp.max(s, axis=-1, keepdims=True)
        p = jnp.exp(s - mx)
        den = jnp.sum(p, axis=-1, keepdims=True)
        o = _dg(p.astype(BF16), vmem)
        out = out + jnp.where(mask, o * (1.0 / den), 0.0)
    return out


def _shift_rows(u, n, carry_rows):
    rolled = pltpu.roll(u, n, axis=0)
    row = lax.broadcasted_iota(jnp.int32, u.shape, 0)
    for i in range(n):
        rolled = jnp.where(row == i, carry_rows[i], rolled)
    return rolled


def _memkv_kernel(mem_ref, g_ref, w_ref, kn_ref, ones_ref, k_out, v_out):
    m = mem_ref[0]
    h = m * _rms_scale(m) * g_ref[0]
    kv = _bdot(h, w_ref[0])
    k = _head_rms(kv[:, :MEM_WIDTH], kn_ref[0], ones_ref[...])
    k_out[0, 0] = k.astype(BF16)
    v_out[0, 0] = kv[:, MEM_WIDTH:].astype(BF16)


def _mem_kv(mem, mem_norm, mem_w_kv, mem_k_norm, ones_bd):
    depth = mem_norm.shape[0]
    B = mem.shape[0]
    kn = jnp.tile(mem_k_norm, (1, MEM_HEADS)).reshape(depth, 1, MEM_WIDTH)
    out_sd = jax.ShapeDtypeStruct((depth, B, N_MEM, MEM_WIDTH), BF16)
    return pl.pallas_call(
        _memkv_kernel,
        grid=(depth, B),
        in_specs=[
            pl.BlockSpec((1, N_MEM, D_MODEL), lambda l, b: (b, 0, 0)),
            pl.BlockSpec((1, 1, D_MODEL), lambda l, b: (l, 0, 0)),
            pl.BlockSpec((1, D_MODEL, 2 * MEM_WIDTH), lambda l, b: (l, 0, 0)),
            pl.BlockSpec((1, 1, MEM_WIDTH), lambda l, b: (l, 0, 0)),
            pl.BlockSpec((MEM_WIDTH, MEM_WIDTH), lambda l, b: (0, 0)),
        ],
        out_specs=[pl.BlockSpec((1, 1, N_MEM, MEM_WIDTH), lambda l, b: (l, b, 0, 0))] * 2,
        out_shape=[out_sd, out_sd],
        name="mem_kv",
    )(mem, mem_norm.reshape(depth, 1, D_MODEL), mem_w_kv.astype(BF16), kn, ones_bd)


def _a_in_kernel(x_ref, g_ref, w_ref, mu_ref, w0_ref, w2_ref, a0_ref, a2_ref, g2_ref, kk_ref, ka_ref,
                 rk_ref, kmem_ref, vmem_ref, qn_ref, ones_ref,
                 r_out, lw_out, k_out, v_out, kk_out, a_out, g_out, bon_out, ym_out, carry_ref):
    W = RWKV_WIDTH

    @pl.when(pl.program_id(1) == 0)
    def _():
        carry_ref[...] = jnp.zeros_like(carry_ref)

    ones = ones_ref[...]
    x = x_ref[0]
    h = x * _rms_scale(x) * g_ref[...]
    p = _bdot(h, w_ref[...])
    ps = p[:, :SHIFT_WIDTH]
    prev = _shift_rows(ps, 1, [carry_ref[0:1, :]])
    carry_ref[0:1, :] = ps[ps.shape[0] - 1:, :]
    xs = ps + (prev - ps) * mu_ref[...]
    r = xs[:, 0:W]
    k = xs[:, W:2 * W]
    v = xs[:, 2 * W:3 * W]
    wa = xs[:, 3 * W:3 * W + DECAY_LORA + AAA_LORA]
    gd = xs[:, 3 * W + DECAY_LORA + AAA_LORA:]
    wl = w0_ref[...] + _dot3(jnp.tanh(wa), w2_ref[...])
    z = -wl
    softplus = jnp.maximum(z, 0.0) + jnp.log(1.0 + jnp.exp(-jnp.abs(z)))
    lw_out[0] = -jnp.exp(-softplus - 0.5)
    a = _sigmoid(a0_ref[...] + _bdot(wa, a2_ref[...]))
    g_out[0] = _bdot(_sigmoid(gd), g2_ref[...])
    kkr = k * kk_ref[...]
    n2 = _seg_sum(kkr * kkr, ones)
    kk_out[0] = kkr * lax.rsqrt(jnp.maximum(n2, 1e-24))
    kmod = k * (1.0 + (a - 1.0) * ka_ref[...])
    bon_out[0] = _seg_sum(r * kmod * rk_ref[...], ones) * v
    r_out[0] = r
    k_out[0] = kmod
    v_out[0] = v
    a_out[0] = a
    ym_out[0] = _mem_attn(p[:, SHIFT_WIDTH:], kmem_ref[0, 0], vmem_ref[0, 0], qn_ref[...], ones)


def _a_in(x, g, w_in, mu, w0, w2, a0, a2, g2, k_k, k_a, r_k, kmem, vmem, layer, qn, ones_bd, tm=256):
    B, T, D = x.shape
    W = RWKV_WIDTH
    zpad = jnp.zeros((DECAY_LORA, W), F32)
    w2p = jnp.concatenate([w2, zpad], axis=0)
    a2p = jnp.concatenate([zpad, a2], axis=0).astype(BF16)
    row = lambda a: a.reshape(1, -1)
    const = lambda shape: pl.BlockSpec(shape, lambda b, t: (0,) * len(shape))
    tok = lambda w: pl.BlockSpec((1, tm, w), lambda b, t: (b, t, 0))
    memspec = pl.BlockSpec((1, 1, N_MEM, MEM_WIDTH), lambda b, t: (layer, b, 0, 0))
    sd = lambda w: jax.ShapeDtypeStruct((B, T, w), F32)
    return pl.pallas_call(
        _a_in_kernel,
        grid=(B, T // tm),
        in_specs=[tok(D), const((1, D)), const((D, A_IN_WIDTH)), const((1, SHIFT_WIDTH)),
                  const((1, W)), const((PAIR, W)), const((1, W)), const((PAIR, W)), const((GATE_LORA, W)),
                  const((1, W)), const((1, W)), const((1, W)), memspec, memspec,
                  const((1, MEM_WIDTH)), const((MEM_WIDTH, MEM_WIDTH))],
        out_specs=[tok(W)] * 8 + [tok(MEM_WIDTH)],
        out_shape=[sd(W)] * 8 + [sd(MEM_WIDTH)],
        scratch_shapes=[pltpu.VMEM((8, SHIFT_WIDTH), F32)],
        compiler_params=pltpu.CompilerParams(
            dimension_semantics=("parallel", "arbitrary"), vmem_limit_bytes=VMEM_LIMIT),
        name="a_in",
    )(x, row(g), w_in.astype(BF16), row(mu), row(w0), w2p, row(a0), a2p, g2.astype(BF16),
      row(k_k), row(k_a), row(r_k), kmem, vmem, qn, ones_bd)


def _scan_kernel(r_ref, lw_ref, k_ref, v_ref, kk_ref, a_ref, g_ref, bon_ref, lnw_ref, lnb_ref,
                 o_ref, h_ref, *, n_chunks):
    C = CHUNK

    @pl.when(pl.program_id(2) == 0)
    def _():
        h_ref[...] = jnp.zeros_like(h_ref)

    ri = lax.broadcasted_iota(jnp.int32, (PAIR, PAIR), 0)
    ci = lax.broadcasted_iota(jnp.int32, (PAIR, PAIR), 1)
    head0 = lax.broadcasted_iota(jnp.int32, (1, PAIR), 1) < HEAD_DIM
    strict = ri > ci
    incl = ri >= ci
    same = (ri >= HEAD_DIM) == (ci >= HEAD_DIM)
    eye = (ri == ci).astype(F32)
    ones_bd = same.astype(BF16)
    tril = (lax.broadcasted_iota(jnp.int32, (C, C), 0) >= lax.broadcasted_iota(jnp.int32, (C, C), 1)).astype(BF16)

    def level_mask(b):
        sh = b.bit_length()
        return ((ri & b) != 0) & ((ci & b) == 0) & ((ri >> sh) == (ci >> sh))

    def ms(z):
        return jnp.concatenate([jnp.where(head0, z, 0.0), jnp.where(head0, 0.0, z)], axis=0)

    def fold(z):
        return z[:C] + z[C:]

    H = h_ref[...]
    for c in range(n_chunks):
        sl = slice(c * C, (c + 1) * C)
        rc, lwc, kc, vc, kkc, ac = (ref[0, sl, :] for ref in (r_ref, lw_ref, k_ref, v_ref, kk_ref, a_ref))
        l1 = lwc.astype(BF16)
        rem = lwc - l1.astype(F32)
        l2 = rem.astype(BF16)
        l3 = (rem - l2.astype(F32)).astype(BF16)
        cl = _dg(tril, l1) + _dg(tril, l2) + _dg(tril, l3)
        clast = cl[C - 1:C, :]
        p_inc = jnp.exp(cl)
        p_exc = jnp.exp(cl - lwc)
        p_inv = jnp.exp(-cl)
        p_end = jnp.exp(clast - cl)
        kkp = kkc * p_exc
        rp = rc * p_inc
        b = kkc * ac
        bd = b * p_inv
        kd = kc * p_inv
        bt = b * p_end
        kt = kc * p_end
        m = _dot3(jnp.concatenate([ms(kkp), ms(rp)], axis=0),
                  jnp.concatenate([ms(bd), ms(kd)], axis=0), NT)
        a_ab = jnp.where(strict, m[:PAIR, :PAIR], 0.0)
        a_ak = jnp.where(strict, m[:PAIR, PAIR:], 0.0)
        b_rb = jnp.where(incl, m[PAIR:, :PAIR], 0.0)
        b_rk = jnp.where(incl, m[PAIR:, PAIR:], 0.0)
        x = eye - jnp.where(level_mask(1), a_ab, 0.0)
        bsz = 2
        while bsz < C:
            e = jnp.where(level_mask(bsz), a_ab, 0.0)
            x = x - _bdot(_bdot(x, e), x)
            bsz *= 2
        av_y = _dot3(jnp.concatenate([a_ak, b_rk], axis=0), ms(vc))
        av = fold(av_y[:PAIR])
        y_loc = fold(av_y[PAIR:])
        wu = _dot3(x, jnp.concatenate([ms(kkp), ms(av)], axis=1))
        w = fold(wu[:, :PAIR])
        ut = fold(wu[:, PAIR:])
        bw = _dot3(b_rb, jnp.concatenate([ms(w), ms(ut)], axis=1))
        rpp = rp - fold(bw[:, :PAIR])
        ypp = y_loc - fold(bw[:, PAIR:])
        btwu = _dot3(bt.T, jnp.concatenate([w, ut], axis=1))
        g_mat = jnp.where(same, eye * jnp.exp(clast) - btwu[:, :PAIR], 0.0)
        f_mat = jnp.where(same, _dot3(kt.T, vc) - btwu[:, PAIR:], 0.0)
        yh = _dot3(jnp.concatenate([rpp, g_mat], axis=0), H)
        y = yh[:C] + ypp
        H = yh[C:] + f_mat
        mean = _seg_sum(y, ones_bd) * (1.0 / HEAD_DIM)
        d = y - mean
        var = _seg_sum(d * d, ones_bd) * (1.0 / HEAD_DIM)
        yn = d * lax.rsqrt(var + LNX_EPS) * lnw_ref[...] + lnb_ref[...]
        o_ref[0, sl, :] = (yn + bon_ref[0, sl, :]) * g_ref[0, sl, :]
    h_ref[...] = H


def _scan(r, lw, k, v, kk, a, g, bon, lnx_w, lnx_b, tb=256):
    B, T, W = r.shape
    tok = pl.BlockSpec((1, tb, PAIR), lambda b, p, t: (b, t, p))
    par = pl.BlockSpec((1, PAIR), lambda b, p, t: (0, p))
    return pl.pallas_call(
        functools.partial(_scan_kernel, n_chunks=tb // CHUNK),
        grid=(B, W // PAIR, T // tb),
        in_specs=[tok] * 8 + [par, par],
        out_specs=tok,
        out_shape=jax.ShapeDtypeStruct((B, T, W), F32),
        scratch_shapes=[pltpu.VMEM((PAIR, PAIR), F32)],
        compiler_params=pltpu.CompilerParams(
            dimension_semantics=("parallel", "parallel", "arbitrary"), vmem_limit_bytes=VMEM_LIMIT),
        name="rwkv_scan",
    )(r, lw, k, v, kk, a, g, bon, lnx_w.reshape(1, W), lnx_b.reshape(1, W))


def _a_out_kernel(ymix_ref, ymem_ref, x_ref, w1_ref, w2_ref, o_ref):
    o_ref[...] = x_ref[...] + _bdot(ymix_ref[...], w1_ref[...]) + _bdot(ymem_ref[...], w2_ref[...])


def _a_out(ymix, ymem, x, w_out, tm=512):
    B, T, D = x.shape
    n = B * T
    wb = w_out.astype(BF16)
    tok = lambda w: pl.BlockSpec((tm, w), lambda i: (i, 0))
    const = lambda shape: pl.BlockSpec(shape, lambda i: (0, 0))
    out = pl.pallas_call(
        _a_out_kernel,
        grid=(n // tm,),
        in_specs=[tok(RWKV_WIDTH), tok(MEM_WIDTH), tok(D), const((RWKV_WIDTH, D)), const((MEM_WIDTH, D))],
        out_specs=tok(D),
        out_shape=jax.ShapeDtypeStruct((n, D), F32),
        compiler_params=pltpu.CompilerParams(dimension_semantics=("parallel",), vmem_limit_bytes=VMEM_LIMIT),
        name="a_out",
    )(ymix.reshape(n, -1), ymem.reshape(n, -1), x.reshape(n, D), wb[:RWKV_WIDTH], wb[RWKV_WIDTH:])
    return out.reshape(B, T, D)


def _b_out_kernel(o1_ref, o2_ref, o3_ref, l1_ref, l2_ref, l3_ref, ymem_ref, x_ref, w1_ref, w2_ref, o_ref):
    l1, l2, l3 = l1_ref[...], l2_ref[...], l3_ref[...]
    mx = jnp.maximum(jnp.maximum(l1, l2), l3)
    e1, e2, e3 = jnp.exp(l1 - mx), jnp.exp(l2 - mx), jnp.exp(l3 - mx)
    mix = (e1 * o1_ref[...] + e2 * o2_ref[...] + e3 * o3_ref[...]) * (1.0 / (e1 + e2 + e3))
    o_ref[...] = x_ref[...] + _bdot(mix, w1_ref[...]) + _bdot(ymem_ref[...], w2_ref[...])


def _b_out(outs, lses, ymem, x, w_out, tm=512):
    B, T, D = x.shape
    n = B * T
    wb = w_out.astype(BF16)
    tok = lambda w: pl.BlockSpec((tm, w), lambda i: (i, 0))
    const = lambda shape: pl.BlockSpec(shape, lambda i: (0, 0))
    flat = lambda z: z.reshape(n, -1)
    out = pl.pallas_call(
        _b_out_kernel,
        grid=(n // tm,),
        in_specs=[tok(GROUP_WIDTH)] * 7 + [tok(D), const((GROUP_WIDTH, D)), const((MEM_WIDTH, D))],
        out_specs=tok(D),
        out_shape=jax.ShapeDtypeStruct((n, D), F32),
        compiler_params=pltpu.CompilerParams(dimension_semantics=("parallel",), vmem_limit_bytes=VMEM_LIMIT),
        name="b_out",
    )(*[flat(o) for o in outs], *[flat(l) for l in lses], flat(ymem), flat(x), wb[:GROUP_WIDTH], wb[GROUP_WIDTH:])
    return out.reshape(B, T, D)


FFN_COLS = 256


def _ffn_kernel(x_ref, g_ref, wg_ref, wv_ref, cwg_ref, cwv_ref, cbg_ref, cbv_ref, wd_ref,
                o_ref, h_ref, carry_ref):
    t = pl.program_id(1)
    j = pl.program_id(2)

    @pl.when(j == 0)
    def _():
        x = x_ref[0]
        h_ref[...] = (x * _rms_scale(x) * g_ref[...]).astype(BF16)
        o_ref[0] = x

    @pl.when(t == 0)
    def _():
        carry_ref[j] = jnp.zeros(carry_ref.shape[1:], F32)

    h = h_ref[...]
    tm = h.shape[0]

    def conv(u, cw_ref, cb_ref, col):
        c0 = carry_ref[j, 0:1, col:col + FFN_COLS]
        c1 = carry_ref[j, 1:2, col:col + FFN_COLS]
        u1 = _shift_rows(u, 1, [c1])
        u2 = _shift_rows(u, 2, [c0, c1])
        carry_ref[j, 0:2, col:col + FFN_COLS] = u[tm - 2:, :]
        return cb_ref[...] + cw_ref[0:1, :] * u2 + cw_ref[1:2, :] * u1 + cw_ref[2:3, :] * u

    cg = conv(_dg(h, wg_ref[...]), cwg_ref, cbg_ref, 0)
    cv = conv(_dg(h, wv_ref[...]), cwv_ref, cbv_ref, FFN_COLS)
    z = cg * _sigmoid(cg) * cv
    o_ref[0] += _bdot(z, wd_ref[...])


def _ffn(x, g, w_up, conv_w, conv_b, w_down, tm=512):
    B, T, D = x.shape
    nj = D_FF // FFN_COLS
    wub = w_up.astype(BF16)
    cb = conv_b.reshape(1, -1)
    return pl.pallas_call(
        _ffn_kernel,
        grid=(B, T // tm, nj),
        in_specs=[
            pl.BlockSpec((1, tm, D), lambda b, t, j: (b, t, 0)),
            pl.BlockSpec((1, D), lambda b, t, j: (0, 0)),
            pl.BlockSpec((D, FFN_COLS), lambda b, t, j: (0, j)),
            pl.BlockSpec((D, FFN_COLS), lambda b, t, j: (0, nj + j)),
            pl.BlockSpec((3, FFN_COLS), lambda b, t, j: (0, j)),
            pl.BlockSpec((3, FFN_COLS), lambda b, t, j: (0, nj + j)),
            pl.BlockSpec((1, FFN_COLS), lambda b, t, j: (0, j)),
            pl.BlockSpec((1, FFN_COLS), lambda b, t, j: (0, nj + j)),
            pl.BlockSpec((FFN_COLS, D), lambda b, t, j: (j, 0)),
        ],
        out_specs=pl.BlockSpec((1, tm, D), lambda b, t, j: (b, t, 0)),
        out_shape=jax.ShapeDtypeStruct((B, T, D), F32),
        scratch_shapes=[pltpu.VMEM((tm, D), BF16), pltpu.VMEM((nj, 8, 2 * FFN_COLS), F32)],
        compiler_params=pltpu.CompilerParams(
            dimension_semantics=("parallel", "arbitrary", "arbitrary"), vmem_limit_bytes=VMEM_LIMIT),
        name="conv_ffn",
    )(x, g.reshape(1, D), wub, wub, conv_w, conv_w, cb, cb, w_down.astype(BF16))


def _rope(z, cos, sin_signed):
    lane = lax.broadcasted_iota(jnp.int32, (1, LANES), 1)
    first_half = (lane % HEAD_DIM) < HEAD_DIM // 2
    outs = []
    for j in range(z.shape[1] // LANES):
        zs = z[:, j * LANES:(j + 1) * LANES]
        rot = jnp.where(first_half, pltpu.roll(zs, LANES - HEAD_DIM // 2, axis=1),
                        pltpu.roll(zs, HEAD_DIM // 2, axis=1))
        outs.append(zs * cos + rot * sin_signed)
    return jnp.concatenate(outs, axis=1)


def _b_in_kernel(x_ref, gkv_ref, gq_ref, wkv_ref, wq_ref, kn_ref, qn_ref, cos_ref, sin_ref,
                 kmem_ref, vmem_ref, mqn_ref, ones_ref, q_out, k_out, v_out, ym_out):
    ones = ones_ref[...]
    x = x_ref[0]
    xn = x * _rms_scale(x)
    kvp = _bdot(xn * gkv_ref[...], wkv_ref[...])
    p = _bdot(xn * gq_ref[...], wq_ref[...])
    cos, sin = cos_ref[...], sin_ref[...]
    k_out[0] = _rope(_head_rms(kvp[:, :DIL_WIDTH], kn_ref[...], ones), cos, sin)
    v_out[0] = kvp[:, DIL_WIDTH:]
    q = _rope(_head_rms(p[:, :DIL_WIDTH], qn_ref[...], ones), cos, sin)
    q_out[0] = q * (1.0 / math.sqrt(HEAD_DIM))
    ym_out[0] = _mem_attn(p[:, DIL_WIDTH:], kmem_ref[0, 0], vmem_ref[0, 0], mqn_ref[...], ones)


def _b_in(x, g_kv, g_q, kv_w, w_in, k_norm, q_norm, cos, sin, kmem, vmem, layer, mqn, ones_bd, tm=256):
    B, T, D = x.shape
    const = lambda shape: pl.BlockSpec(shape, lambda b, t: (0,) * len(shape))
    tok = lambda w: pl.BlockSpec((1, tm, w), lambda b, t: (b, t, 0))
    tab = pl.BlockSpec((tm, LANES), lambda b, t: (t, 0))
    memspec = pl.BlockSpec((1, 1, N_MEM, MEM_WIDTH), lambda b, t: (layer, b, 0, 0))
    sd = lambda w: jax.ShapeDtypeStruct((B, T, w), F32)
    tile_heads = lambda gain: jnp.tile(gain, DIL_WIDTH // HEAD_DIM).reshape(1, DIL_WIDTH)
    return pl.pallas_call(
        _b_in_kernel,
        grid=(B, T // tm),
        in_specs=[tok(D), const((1, D)), const((1, D)), const((D, 2 * DIL_WIDTH)),
                  const((D, DIL_WIDTH + MEM_WIDTH)), const((1, DIL_WIDTH)), const((1, DIL_WIDTH)),
                  tab, tab, memspec, memspec, const((1, MEM_WIDTH)), const((MEM_WIDTH, MEM_WIDTH))],
        out_specs=[tok(DIL_WIDTH)] * 3 + [tok(MEM_WIDTH)],
        out_shape=[sd(DIL_WIDTH)] * 3 + [sd(MEM_WIDTH)],
        compiler_params=pltpu.CompilerParams(
            dimension_semantics=("parallel", "parallel"), vmem_limit_bytes=VMEM_LIMIT),
        name="b_in",
    )(x, g_kv.reshape(1, D), g_q.reshape(1, D), kv_w.astype(BF16), w_in.astype(BF16),
      tile_heads(k_norm), tile_heads(q_norm), cos, sin, kmem, vmem, mqn, ones_bd)


def _dil_kernel(q_ref, kc_ref, kp_ref, vc_ref, vp_ref, o_ref, l_ref, *, dilation):
    n = pl.program_id(2)
    blk = DIL_BLOCK
    i = lax.broadcasted_iota(jnp.int32, (blk, 2 * blk), 0)
    jj = lax.broadcasted_iota(jnp.int32, (blk, 2 * blk), 1)
    dist = blk + i - jj
    valid = (dist >= 0) & (dist <= blk) & ((n > 0) | (jj >= blk))
    head0 = lax.broadcasted_iota(jnp.int32, (1, PAIR), 1) < HEAD_DIM
    for r in range(dilation):
        rows = pl.ds(r, blk, stride=dilation) if dilation > 1 else pl.ds(0, blk)
        q = q_ref[0, rows, :]
        kw = jnp.concatenate([kp_ref[0, rows, :], kc_ref[0, rows, :]], axis=0).astype(BF16)
        vw = jnp.concatenate([vp_ref[0, rows, :], vc_ref[0, rows, :]], axis=0).astype(BF16)
        out = jnp.zeros((blk, PAIR), F32)
        lse = jnp.zeros((blk, PAIR), F32)
        for mask in (head0, jnp.logical_not(head0)):
            s = _dg(jnp.where(mask, q, 0.0).astype(BF16), kw, NT)
            s = jnp.where(valid, s, NEG_INF)
            mx = jnp.max(s, axis=-1, keepdims=True)
            p = jnp.exp(s - mx)
            den = jnp.sum(p, axis=-1, keepdims=True)
            o = _dg(p.astype(BF16), vw)
            out = out + jnp.where(mask, o * (1.0 / den), 0.0)
            lse = lse + jnp.where(mask, mx + jnp.log(den), 0.0)
        o_ref[0, rows, :] = out
        l_ref[0, rows, :] = lse


def _dilated_group(q, k, v, gi, dilation):
    B, T, _ = q.shape
    span = dilation * DIL_BLOCK
    pairs = GROUP_WIDTH // PAIR
    cur = pl.BlockSpec((1, span, PAIR), lambda b, p, n: (b, n, gi * pairs + p))
    prev = pl.BlockSpec((1, span, PAIR), lambda b, p, n: (b, jnp.maximum(n - 1, 0), gi * pairs + p))
    out = pl.BlockSpec((1, span, PAIR), lambda b, p, n: (b, n, p))
    sd = jax.ShapeDtypeStruct((B, T, GROUP_WIDTH), F32)
    return pl.pallas_call(
        functools.partial(_dil_kernel, dilation=dilation),
        grid=(B, pairs, T // span),
        in_specs=[cur, cur, prev, cur, prev],
        out_specs=[out, out],
        out_shape=[sd, sd],
        compiler_params=pltpu.CompilerParams(
            dimension_semantics=("parallel", "parallel", "parallel"), vmem_limit_bytes=VMEM_LIMIT),
        name=f"dilated_d{dilation}",
    )(q, k, k, v, v)


def _rope_tables(T):
    inv = ROPE_THETA ** (-jnp.arange(0, HEAD_DIM, 2, dtype=F32) / HEAD_DIM)
    ang = jnp.arange(T, dtype=F32)[:, None] * inv[None, :]
    cos, sin = jnp.cos(ang), jnp.sin(ang)
    cos_t = jnp.tile(cos, (1, LANES // (HEAD_DIM // 2)))
    sin_t = jnp.tile(jnp.concatenate([-sin, sin], axis=1), (1, LANES // HEAD_DIM))
    return cos_t, sin_t


def kernel(x, mem, attn_norm, a_w_in, a_mu, a_w0, a_w2, a_a0, a_a2, a_g2, a_k_k, a_k_a, a_r_k, a_lnx_w, a_lnx_b, a_w_out, kv_norm, kv_w, kv_k_norm, b_w_in, b_q_norm, b_w_out, mem_norm, mem_w_kv, mem_q_norm, mem_k_norm, ffn_norm, ffn_w_up, ffn_conv_w, ffn_conv_b, ffn_w_down):
    B, T, D = x.shape
    depth = attn_norm.shape[0]
    n_a = a_w_in.shape[0]
    seg = jnp.arange(MEM_WIDTH) // HEAD_DIM
    ones_bd = (seg[:, None] == seg[None, :]).astype(BF16)
    kmem, vmem = _mem_kv(mem, mem_norm, mem_w_kv, mem_k_norm, ones_bd)
    cos_t, sin_t = _rope_tables(T)
    k_sh = v_sh = None
    for i in range(depth):
        mqn = jnp.tile(mem_q_norm[i], MEM_HEADS).reshape(1, MEM_WIDTH)
        if i < n_a:
            j = i
            r, lw, k, v, kk, a, g, bon, ymem = _a_in(
                x, attn_norm[i], a_w_in[j], a_mu[j], a_w0[j], a_w2[j], a_a0[j], a_a2[j], a_g2[j],
                a_k_k[j], a_k_a[j], a_r_k[j].reshape(-1), kmem, vmem, i, mqn, ones_bd)
            ymix = _scan(r, lw, k, v, kk, a, g, bon, a_lnx_w[j], a_lnx_b[j])
            x = _a_out(ymix, ymem, x, a_w_out[j])
        else:
            j = i - n_a
            q, k_new, v_new, ymem = _b_in(x, kv_norm, attn_norm[i], kv_w, b_w_in[j], kv_k_norm, b_q_norm[j],
                                          cos_t, sin_t, kmem, vmem, i, mqn, ones_bd)
            if j == 0:
                k_sh, v_sh = k_new, v_new
            outs, lses = [], []
            for gi, (_, dil) in enumerate(DIL_GROUPS):
                o, l = _dilated_group(q, k_sh, v_sh, gi, dil)
                outs.append(o)
                lses.append(l)
            x = _b_out(outs, lses, ymem, x, b_w_out[j])
        x = _ffn(x, ffn_norm[i], ffn_w_up[i], ffn_conv_w[i], ffn_conv_b[i], ffn_w_down[i])
    return x
```

```python
import functools
import math

import jax
import jax.numpy as jnp
from jax import lax
from jax.experimental import pallas as pl
from jax.experimental.pallas import tpu as pltpu

F32 = jnp.float32
BF16 = jnp.bfloat16

D_MODEL = 1024
HEAD_DIM = 64
N_MEM = 256
MEM_HEADS = 4
MEM_WIDTH = MEM_HEADS * HEAD_DIM
RWKV_HEADS = 12
RWKV_WIDTH = RWKV_HEADS * HEAD_DIM
DECAY_LORA = 64
AAA_LORA = 64
GATE_LORA = 128
SHIFT_WIDTH = 3 * RWKV_WIDTH + DECAY_LORA + AAA_LORA + GATE_LORA
A_IN_WIDTH = SHIFT_WIDTH + MEM_WIDTH
DIL_GROUPS = ((128, 1), (512, 4), (2048, 16))
DIL_HEADS_PER_GROUP = 4
GROUP_WIDTH = DIL_HEADS_PER_GROUP * HEAD_DIM
DIL_WIDTH = GROUP_WIDTH * len(DIL_GROUPS)
DIL_BLOCK = 128
D_FF = 2816
ROPE_THETA = 10000.0
RMS_EPS = 1e-6
LNX_EPS = 64e-5
NEG_INF = -1e30

LANES = 128
PAIR = 2 * HEAD_DIM
CHUNK = 64
VMEM_LIMIT = 52 * 1024 * 1024

NN = ((1,), (0,))
NT = ((1,), (1,))
TN = ((0,), (0,))


def _dg(a, b, dims=NN):
    return lax.dot_general(a, b, (dims, ((), ())), preferred_element_type=F32)


def _bdot(a, b, dims=NN):
    return _dg(a.astype(BF16), b.astype(BF16), dims)


def _split2(x):
    hi = x.astype(BF16)
    lo = (x - hi.astype(F32)).astype(BF16)
    return hi, lo


def _dot3(a, b, dims=NN):
    ah, al = _split2(a)
    bh, bl = _split2(b)
    return _dg(ah, bh, dims) + _dg(ah, bl, dims) + _dg(al, bh, dims)


def _seg_sum(x, ones_bd):
    w = ones_bd.shape[0]
    outs = []
    for j in range(x.shape[1] // w):
        hi, lo = _split2(x[:, j * w:(j + 1) * w])
        outs.append(_dg(hi, ones_bd) + _dg(lo, ones_bd))
    return outs[0] if len(outs) == 1 else jnp.concatenate(outs, axis=1)


def _rms_scale(x):
    return lax.rsqrt(jnp.mean(x * x, axis=-1, keepdims=True) + RMS_EPS)


def _sigmoid(z):
    return 1.0 / (1.0 + jnp.exp(-z))


def _head_rms(z, gain_tiled, ones_bd):
    ms = _seg_sum(z * z, ones_bd) * (1.0 / HEAD_DIM)
    return z * lax.rsqrt(ms + RMS_EPS) * gain_tiled


def _mem_attn(qm, kmem, vmem, qn_tiled, ones_bd):
    qh = _head_rms(qm, qn_tiled, ones_bd) * (1.0 / math.sqrt(HEAD_DIM))
    lane = lax.broadcasted_iota(jnp.int32, (1, MEM_WIDTH), 1)
    out = jnp.zeros_like(qm)
    for h in range(MEM_HEADS):
        mask = (lane >= h * HEAD_DIM) & (lane < (h + 1) * HEAD_DIM)
        s = _dg(jnp.where(mask, qh, 0.0).astype(BF16), kmem, NT)
        mx = jnp.max(s, axis=-1, keepdims=True)
        p = jnp.exp(s - mx)
        den = jnp.sum(p, axis=-1, keepdims=True)
        o = _dg(p.astype(BF16), vmem)
        out = out + jnp.where(mask, o * (1.0 / den), 0.0)
    return out


def _shift_rows(u, n, carry_rows):
    rolled = pltpu.roll(u, n, axis=0)
    head = rolled[:8]
    row = lax.broadcasted_iota(jnp.int32, head.shape, 0)
    for i in range(n):
        head = jnp.where(row == i, carry_rows[i], head)
    return jnp.concatenate([head, rolled[8:]], axis=0)


def _memkv_kernel(mem_ref, g_ref, w_ref, kn_ref, ones_ref, k_out, v_out):
    m = mem_ref[0]
    h = m * _rms_scale(m) * g_ref[0]
    kv = _bdot(h, w_ref[0])
    k = _head_rms(kv[:, :MEM_WIDTH], kn_ref[0], ones_ref[...])
    k_out[0, 0] = k.astype(BF16)
    v_out[0, 0] = kv[:, MEM_WIDTH:].astype(BF16)


def _mem_kv(mem, mem_norm, mem_w_kv, mem_k_norm, ones_bd):
    depth = mem_norm.shape[0]
    B = mem.shape[0]
    kn = jnp.tile(mem_k_norm, (1, MEM_HEADS)).reshape(depth, 1, MEM_WIDTH)
    out_sd = jax.ShapeDtypeStruct((depth, B, N_MEM, MEM_WIDTH), BF16)
    return pl.pallas_call(
        _memkv_kernel,
        grid=(depth, B),
        in_specs=[
            pl.BlockSpec((1, N_MEM, D_MODEL), lambda l, b: (b, 0, 0)),
            pl.BlockSpec((1, 1, D_MODEL), lambda l, b: (l, 0, 0)),
            pl.BlockSpec((1, D_MODEL, 2 * MEM_WIDTH), lambda l, b: (l, 0, 0)),
            pl.BlockSpec((1, 1, MEM_WIDTH), lambda l, b: (l, 0, 0)),
            pl.BlockSpec((MEM_WIDTH, MEM_WIDTH), lambda l, b: (0, 0)),
        ],
        out_specs=[pl.BlockSpec((1, 1, N_MEM, MEM_WIDTH), lambda l, b: (l, b, 0, 0))] * 2,
        out_shape=[out_sd, out_sd],
        name="mem_kv",
    )(mem, mem_norm.reshape(depth, 1, D_MODEL), mem_w_kv.astype(BF16), kn, ones_bd)


def _a_in_kernel(x_ref, g_ref, w_ref, mu_ref, w0_ref, w2_ref, a0_ref, a2_ref, g2_ref, kk_ref, ka_ref,
                 rk_ref, kmem_ref, vmem_ref, qn_ref, ones_ref,
                 r_out, lw_out, k_out, v_out, kk_out, a_out, g_out, bon_out, ym_out, carry_ref):
    W = RWKV_WIDTH

    @pl.when(pl.program_id(1) == 0)
    def _():
        carry_ref[...] = jnp.zeros_like(carry_ref)

    ones = ones_ref[...]
    x = x_ref[0]
    h = x * _rms_scale(x) * g_ref[...]
    p = _bdot(h, w_ref[...])
    ps = p[:, :SHIFT_WIDTH]
    prev = _shift_rows(ps, 1, [carry_ref[0:1, :]])
    carry_ref[0:1, :] = ps[ps.shape[0] - 1:, :]
    xs = ps + (prev - ps) * mu_ref[...]
    r = xs[:, 0:W]
    k = xs[:, W:2 * W]
    v = xs[:, 2 * W:3 * W]
    wa = xs[:, 3 * W:3 * W + DECAY_LORA + AAA_LORA]
    gd = xs[:, 3 * W + DECAY_LORA + AAA_LORA:]
    wl = w0_ref[...] + _dot3(jnp.tanh(wa), w2_ref[...])
    z = -wl
    softplus = jnp.maximum(z, 0.0) + jnp.log(1.0 + jnp.exp(-jnp.abs(z)))
    lw_out[0] = -jnp.exp(-softplus - 0.5)
    a = _sigmoid(a0_ref[...] + _bdot(wa, a2_ref[...]))
    g_out[0] = _bdot(_sigmoid(gd), g2_ref[...])
    kkr = k * kk_ref[...]
    n2 = _seg_sum(kkr * kkr, ones)
    kk_out[0] = kkr * lax.rsqrt(jnp.maximum(n2, 1e-24))
    kmod = k * (1.0 + (a - 1.0) * ka_ref[...])
    bon_out[0] = _seg_sum(r * kmod * rk_ref[...], ones) * v
    r_out[0] = r
    k_out[0] = kmod
    v_out[0] = v
    a_out[0] = a
    ym_out[0] = _mem_attn(p[:, SHIFT_WIDTH:], kmem_ref[0, 0], vmem_ref[0, 0], qn_ref[...], ones)


def _a_in(x, g, w_in, mu, w0, w2, a0, a2, g2, k_k, k_a, r_k, kmem, vmem, layer, qn, ones_bd, tm=256):
    B, T, D = x.shape
    W = RWKV_WIDTH
    zpad = jnp.zeros((DECAY_LORA, W), F32)
    w2p = jnp.concatenate([w2, zpad], axis=0)
    a2p = jnp.concatenate([zpad, a2], axis=0).astype(BF16)
    row = lambda a: a.reshape(1, -1)
    const = lambda shape: pl.BlockSpec(shape, lambda b, t: (0,) * len(shape))
    tok = lambda w: pl.BlockSpec((1, tm, w), lambda b, t: (b, t, 0))
    memspec = pl.BlockSpec((1, 1, N_MEM, MEM_WIDTH), lambda b, t: (layer, b, 0, 0))
    sd = lambda w: jax.ShapeDtypeStruct((B, T, w), F32)
    return pl.pallas_call(
        _a_in_kernel,
        grid=(B, T // tm),
        in_specs=[tok(D), const((1, D)), const((D, A_IN_WIDTH)), const((1, SHIFT_WIDTH)),
                  const((1, W)), const((PAIR, W)), const((1, W)), const((PAIR, W)), const((GATE_LORA, W)),
                  const((1, W)), const((1, W)), const((1, W)), memspec, memspec,
                  const((1, MEM_WIDTH)), const((MEM_WIDTH, MEM_WIDTH))],
        out_specs=[tok(W)] * 8 + [tok(MEM_WIDTH)],
        out_shape=[sd(W)] * 8 + [sd(MEM_WIDTH)],
        scratch_shapes=[pltpu.VMEM((8, SHIFT_WIDTH), F32)],
        compiler_params=pltpu.CompilerParams(
            dimension_semantics=("parallel", "arbitrary"), vmem_limit_bytes=VMEM_LIMIT),
        name="a_in",
    )(x, row(g), w_in.astype(BF16), row(mu), row(w0), w2p, row(a0), a2p, g2.astype(BF16),
      row(k_k), row(k_a), row(r_k), kmem, vmem, qn, ones_bd)


def _scan_kernel(r_ref, lw_ref, k_ref, v_ref, kk_ref, a_ref, g_ref, bon_ref, lnw_ref, lnb_ref,
                 o_ref, h_ref, *, n_chunks, n_pairs):
    C = CHUNK

    @pl.when(pl.program_id(2) == 0)
    def _():
        h_ref[...] = jnp.zeros_like(h_ref)

    ri = lax.broadcasted_iota(jnp.int32, (PAIR, PAIR), 0)
    ci = lax.broadcasted_iota(jnp.int32, (PAIR, PAIR), 1)
    head0 = lax.broadcasted_iota(jnp.int32, (1, PAIR), 1) < HEAD_DIM
    strict = ri > ci
    incl = ri >= ci
    same = (ri >= HEAD_DIM) == (ci >= HEAD_DIM)
    eye = (ri == ci).astype(F32)
    ones_bd = same.astype(BF16)
    tril = (lax.broadcasted_iota(jnp.int32, (C, C), 0) >= lax.broadcasted_iota(jnp.int32, (C, C), 1)).astype(BF16)

    def level_mask(b):
        sh = b.bit_length()
        return ((ri & b) != 0) & ((ci & b) == 0) & ((ri >> sh) == (ci >> sh))

    def ms(z):
        return jnp.concatenate([jnp.where(head0, z, 0.0), jnp.where(head0, 0.0, z)], axis=0)

    def fold(z):
        return z[:C] + z[C:]

    items = [(p, c) for p in range(n_pairs) for c in range(n_chunks)]
    n = len(items)

    def tile(ref, it):
        p, c = it
        return ref[0, c * C:(c + 1) * C, p * PAIR:(p + 1) * PAIR]

    rp, vcs, lhs, rhs, bt_t, kt_t, p_tot = [], [], [], [], [], [], []
    for it in items:
        rc, lwc, kc, vc, kkc, ac = (tile(ref, it) for ref in (r_ref, lw_ref, k_ref, v_ref, kk_ref, a_ref))
        l1 = lwc.astype(BF16)
        rem = lwc - l1.astype(F32)
        l2 = rem.astype(BF16)
        l3 = (rem - l2.astype(F32)).astype(BF16)
        cl = _dg(tril, l1) + _dg(tril, l2) + _dg(tril, l3)
        clast = cl[C - 1:C, :]
        p_inv = jnp.exp(-cl)
        p_end = jnp.exp(clast - cl)
        b = kkc * ac
        rpi = rc * jnp.exp(cl)
        rp.append(rpi)
        vcs.append(vc)
        p_tot.append(jnp.exp(clast))
        lhs.append(jnp.concatenate([ms(kkc * jnp.exp(cl - lwc)), ms(rpi)], axis=0).astype(BF16))
        rhs.append(jnp.concatenate([ms(b * p_inv), ms(kc * p_inv)], axis=0).astype(BF16))
        bt_t.append((b * p_end).T.astype(BF16))
        kt_t.append((kc * p_end).T.astype(BF16))
    msv = [ms(vc).astype(BF16) for vc in vcs]

    m = [_dg(lhs[i], rhs[i], NT) for i in range(n)]
    a_ab = [jnp.where(strict, m[i][:PAIR, :PAIR], 0.0) for i in range(n)]
    ak_rk = [jnp.concatenate([jnp.where(strict, m[i][:PAIR, PAIR:], 0.0),
                              jnp.where(incl, m[i][PAIR:, PAIR:], 0.0)], axis=0).astype(BF16) for i in range(n)]
    b_rb = [jnp.where(incl, m[i][PAIR:, :PAIR], 0.0).astype(BF16) for i in range(n)]
    av_y = [_dg(ak_rk[i], msv[i]) for i in range(n)]

    x = [eye - jnp.where(level_mask(1), a_ab[i], 0.0) for i in range(n)]
    bsz = 2
    while bsz < C:
        lm = level_mask(bsz)
        xb = [xi.astype(BF16) for xi in x]
        xe = [_dg(xb[i], jnp.where(lm, a_ab[i], 0.0).astype(BF16)).astype(BF16) for i in range(n)]
        x = [x[i] - _dg(xe[i], xb[i]) for i in range(n)]
        bsz *= 2

    wu = [_dg(x[i].astype(BF16),
              jnp.concatenate([lhs[i][:PAIR], ms(fold(av_y[i][:PAIR])).astype(BF16)], axis=1)) for i in range(n)]
    w = [fold(wu[i][:, :PAIR]) for i in range(n)]
    ut = [fold(wu[i][:, PAIR:]) for i in range(n)]
    bw = [_dg(b_rb[i], jnp.concatenate([ms(w[i]), ms(ut[i])], axis=1).astype(BF16)) for i in range(n)]
    wut = [jnp.concatenate([w[i], ut[i]], axis=1).astype(BF16) for i in range(n)]
    btwu = [_dg(bt_t[i], wut[i]) for i in range(n)]
    ktv = [_dg(kt_t[i], vcs[i].astype(BF16)) for i in range(n)]
    rg = [jnp.concatenate([rp[i] - fold(bw[i][:, :PAIR]),
                           jnp.where(same, eye * p_tot[i] - btwu[i][:, :PAIR], 0.0)], axis=0).astype(BF16)
          for i in range(n)]
    ypp = [fold(av_y[i][PAIR:]) - fold(bw[i][:, PAIR:]) for i in range(n)]
    f_mat = [jnp.where(same, ktv[i] - btwu[i][:, PAIR:], 0.0) for i in range(n)]

    H = [h_ref[p] for p in range(n_pairs)]
    ys = [None] * n
    for c in range(n_chunks):
        for p in range(n_pairs):
            i = p * n_chunks + c
            yh = _dg(rg[i], H[p].astype(BF16))
            ys[i] = yh[:C] + ypp[i]
            H[p] = yh[C:] + f_mat[i]
    for p in range(n_pairs):
        h_ref[p] = H[p]

    for i, it in enumerate(items):
        p, c = it
        y = ys[i]
        mean = _seg_sum(y, ones_bd) * (1.0 / HEAD_DIM)
        d = y - mean
        var = _seg_sum(d * d, ones_bd) * (1.0 / HEAD_DIM)
        lanes = slice(p * PAIR, (p + 1) * PAIR)
        yn = d * lax.rsqrt(var + LNX_EPS) * lnw_ref[:, lanes] + lnb_ref[:, lanes]
        o_ref[0, c * C:(c + 1) * C, lanes] = (yn + tile(bon_ref, it)) * tile(g_ref, it)


def _scan(r, lw, k, v, kk, a, g, bon, lnx_w, lnx_b, tb=256, n_pairs=3):
    B, T, W = r.shape
    wb = n_pairs * PAIR
    tok = pl.BlockSpec((1, tb, wb), lambda b, p, t: (b, t, p))
    par = pl.BlockSpec((1, wb), lambda b, p, t: (0, p))
    return pl.pallas_call(
        functools.partial(_scan_kernel, n_chunks=tb // CHUNK, n_pairs=n_pairs),
        grid=(B, W // wb, T // tb),
        in_specs=[tok] * 8 + [par, par],
        out_specs=tok,
        out_shape=jax.ShapeDtypeStruct((B, T, W), F32),
        scratch_shapes=[pltpu.VMEM((n_pairs, PAIR, PAIR), F32)],
        compiler_params=pltpu.CompilerParams(
            dimension_semantics=("parallel", "parallel", "arbitrary"), vmem_limit_bytes=VMEM_LIMIT),
        name="rwkv_scan",
    )(r, lw, k, v, kk, a, g, bon, lnx_w.reshape(1, W), lnx_b.reshape(1, W))


def _a_out_kernel(ymix_ref, ymem_ref, x_ref, w1_ref, w2_ref, o_ref):
    o_ref[...] = x_ref[...] + _bdot(ymix_ref[...], w1_ref[...]) + _bdot(ymem_ref[...], w2_ref[...])


def _a_out(ymix, ymem, x, w_out, tm=512):
    B, T, D = x.shape
    n = B * T
    wb = w_out.astype(BF16)
    tok = lambda w: pl.BlockSpec((tm, w), lambda i: (i, 0))
    const = lambda shape: pl.BlockSpec(shape, lambda i: (0, 0))
    out = pl.pallas_call(
        _a_out_kernel,
        grid=(n // tm,),
        in_specs=[tok(RWKV_WIDTH), tok(MEM_WIDTH), tok(D), const((RWKV_WIDTH, D)), const((MEM_WIDTH, D))],
        out_specs=tok(D),
        out_shape=jax.ShapeDtypeStruct((n, D), F32),
        compiler_params=pltpu.CompilerParams(dimension_semantics=("parallel",), vmem_limit_bytes=VMEM_LIMIT),
        name="a_out",
    )(ymix.reshape(n, -1), ymem.reshape(n, -1), x.reshape(n, D), wb[:RWKV_WIDTH], wb[RWKV_WIDTH:])
    return out.reshape(B, T, D)


def _b_out_kernel(o1_ref, o2_ref, o3_ref, l1_ref, l2_ref, l3_ref, ymem_ref, x_ref, w1_ref, w2_ref, o_ref):
    l1, l2, l3 = l1_ref[...], l2_ref[...], l3_ref[...]
    mx = jnp.maximum(jnp.maximum(l1, l2), l3)
    e1, e2, e3 = jnp.exp(l1 - mx), jnp.exp(l2 - mx), jnp.exp(l3 - mx)
    mix = (e1 * o1_ref[...] + e2 * o2_ref[...] + e3 * o3_ref[...]) * (1.0 / (e1 + e2 + e3))
    o_ref[...] = x_ref[...] + _bdot(mix, w1_ref[...]) + _bdot(ymem_ref[...], w2_ref[...])


def _b_out(outs, lses, ymem, x, w_out, tm=512):
    B, T, D = x.shape
    n = B * T
    wb = w_out.astype(BF16)
    tok = lambda w: pl.BlockSpec((tm, w), lambda i: (i, 0))
    const = lambda shape: pl.BlockSpec(shape, lambda i: (0, 0))
    flat = lambda z: z.reshape(n, -1)
    out = pl.pallas_call(
        _b_out_kernel,
        grid=(n // tm,),
        in_specs=[tok(GROUP_WIDTH)] * 7 + [tok(D), const((GROUP_WIDTH, D)), const((MEM_WIDTH, D))],
        out_specs=tok(D),
        out_shape=jax.ShapeDtypeStruct((n, D), F32),
        compiler_params=pltpu.CompilerParams(dimension_semantics=("parallel",), vmem_limit_bytes=VMEM_LIMIT),
        name="b_out",
    )(*[flat(o) for o in outs], *[flat(l) for l in lses], flat(ymem), flat(x), wb[:GROUP_WIDTH], wb[GROUP_WIDTH:])
    return out.reshape(B, T, D)


FFN_COLS = 256


def _ffn_kernel(x_ref, g_ref, wup_ref, cw_ref, cb_ref, wd_ref, o_ref, z_ref, carry_ref):
    @pl.when(pl.program_id(1) == 0)
    def _():
        carry_ref[...] = jnp.zeros_like(carry_ref)

    x = x_ref[0]
    h = (x * _rms_scale(x) * g_ref[...]).astype(BF16)
    tm = h.shape[0]

    def conv(col):
        cols = slice(col, col + FFN_COLS)
        u = _dg(h, wup_ref[:, cols])
        c0 = carry_ref[0:1, cols]
        c1 = carry_ref[1:2, cols]
        u1 = _shift_rows(u, 1, [c1])
        u2 = _shift_rows(u, 2, [c0, c1])
        carry_ref[0:2, cols] = u[tm - 2:, :]
        return cb_ref[:, cols] + cw_ref[0:1, cols] * u2 + cw_ref[1:2, cols] * u1 + cw_ref[2:3, cols] * u

    for j in range(D_FF // FFN_COLS):
        cg = conv(j * FFN_COLS)
        cv = conv(D_FF + j * FFN_COLS)
        z_ref[:, j * FFN_COLS:(j + 1) * FFN_COLS] = (cg * _sigmoid(cg) * cv).astype(BF16)
    o_ref[0] = x + _dg(z_ref[...], wd_ref[...])


def _ffn(x, g, w_up, conv_w, conv_b, w_down, tm=512):
    B, T, D = x.shape
    resident = lambda shape: pl.BlockSpec(shape, lambda b, t: (0, 0), pipeline_mode=pl.Buffered(1))
    return pl.pallas_call(
        _ffn_kernel,
        grid=(B, T // tm),
        in_specs=[
            pl.BlockSpec((1, tm, D), lambda b, t: (b, t, 0)),
            resident((1, D)),
            resident((D, 2 * D_FF)),
            resident((3, 2 * D_FF)),
            resident((1, 2 * D_FF)),
            resident((D_FF, D)),
        ],
        out_specs=pl.BlockSpec((1, tm, D), lambda b, t: (b, t, 0)),
        out_shape=jax.ShapeDtypeStruct((B, T, D), F32),
        scratch_shapes=[pltpu.VMEM((tm, D_FF), BF16), pltpu.VMEM((8, 2 * D_FF), F32)],
        compiler_params=pltpu.CompilerParams(
            dimension_semantics=("parallel", "arbitrary"), vmem_limit_bytes=VMEM_LIMIT),
        name="conv_ffn",
    )(x, g.reshape(1, D), w_up.astype(BF16), conv_w, conv_b.reshape(1, -1), w_down.astype(BF16))


def _rope(z, cos, sin_signed):
    lane = lax.broadcasted_iota(jnp.int32, (1, LANES), 1)
    first_half = (lane % HEAD_DIM) < HEAD_DIM // 2
    outs = []
    for j in range(z.shape[1] // LANES):
        zs = z[:, j * LANES:(j + 1) * LANES]
        rot = jnp.where(first_half, pltpu.roll(zs, LANES - HEAD_DIM // 2, axis=1),
                        pltpu.roll(zs, HEAD_DIM // 2, axis=1))
        outs.append(zs * cos + rot * sin_signed)
    return jnp.concatenate(outs, axis=1)


def _b_in_kernel(x_ref, gkv_ref, gq_ref, wkv_ref, wq_ref, kn_ref, qn_ref, cos_ref, sin_ref,
                 kmem_ref, vmem_ref, mqn_ref, ones_ref, q_out, k_out, v_out, ym_out):
    ones = ones_ref[...]
    x = x_ref[0]
    xn = x * _rms_scale(x)
    kvp = _bdot(xn * gkv_ref[...], wkv_ref[...])
    p = _bdot(xn * gq_ref[...], wq_ref[...])
    cos, sin = cos_ref[...], sin_ref[...]
    k_out[0] = _rope(_head_rms(kvp[:, :DIL_WIDTH], kn_ref[...], ones), cos, sin)
    v_out[0] = kvp[:, DIL_WIDTH:]
    q = _rope(_head_rms(p[:, :DIL_WIDTH], qn_ref[...], ones), cos, sin)
    q_out[0] = q * (1.0 / math.sqrt(HEAD_DIM))
    ym_out[0] = _mem_attn(p[:, DIL_WIDTH:], kmem_ref[0, 0], vmem_ref[0, 0], mqn_ref[...], ones)


def _b_in(x, g_kv, g_q, kv_w, w_in, k_norm, q_norm, cos, sin, kmem, vmem, layer, mqn, ones_bd, tm=256):
    B, T, D = x.shape
    const = lambda shape: pl.BlockSpec(shape, lambda b, t: (0,) * len(shape))
    tok = lambda w: pl.BlockSpec((1, tm, w), lambda b, t: (b, t, 0))
    tab = pl.BlockSpec((tm, LANES), lambda b, t: (t, 0))
    memspec = pl.BlockSpec((1, 1, N_MEM, MEM_WIDTH), lambda b, t: (layer, b, 0, 0))
    sd = lambda w: jax.ShapeDtypeStruct((B, T, w), F32)
    tile_heads = lambda gain: jnp.tile(gain, DIL_WIDTH // HEAD_DIM).reshape(1, DIL_WIDTH)
    return pl.pallas_call(
        _b_in_kernel,
        grid=(B, T // tm),
        in_specs=[tok(D), const((1, D)), const((1, D)), const((D, 2 * DIL_WIDTH)),
                  const((D, DIL_WIDTH + MEM_WIDTH)), const((1, DIL_WIDTH)), const((1, DIL_WIDTH)),
                  tab, tab, memspec, memspec, const((1, MEM_WIDTH)), const((MEM_WIDTH, MEM_WIDTH))],
        out_specs=[tok(DIL_WIDTH)] * 3 + [tok(MEM_WIDTH)],
        out_shape=[sd(DIL_WIDTH)] * 3 + [sd(MEM_WIDTH)],
        compiler_params=pltpu.CompilerParams(
            dimension_semantics=("parallel", "parallel"), vmem_limit_bytes=VMEM_LIMIT),
        name="b_in",
    )(x, g_kv.reshape(1, D), g_q.reshape(1, D), kv_w.astype(BF16), w_in.astype(BF16),
      tile_heads(k_norm), tile_heads(q_norm), cos, sin, kmem, vmem, mqn, ones_bd)


def _dil_kernel(q_ref, kc_ref, kp_ref, vc_ref, vp_ref, o_ref, l_ref, *, dilation):
    n = pl.program_id(2)
    blk = DIL_BLOCK
    i = lax.broadcasted_iota(jnp.int32, (blk, 2 * blk), 0)
    jj = lax.broadcasted_iota(jnp.int32, (blk, 2 * blk), 1)
    dist = blk + i - jj
    valid = (dist >= 0) & (dist <= blk) & ((n > 0) | (jj >= blk))
    head0 = lax.broadcasted_iota(jnp.int32, (1, PAIR), 1) < HEAD_DIM
    for r in range(dilation):
        rows = pl.ds(r, blk, stride=dilation) if dilation > 1 else pl.ds(0, blk)
        q = q_ref[0, rows, :]
        kw = jnp.concatenate([kp_ref[0, rows, :], kc_ref[0, rows, :]], axis=0).astype(BF16)
        vw = jnp.concatenate([vp_ref[0, rows, :], vc_ref[0, rows, :]], axis=0).astype(BF16)
        out = jnp.zeros((blk, PAIR), F32)
        lse = jnp.zeros((blk, PAIR), F32)
        for mask in (head0, jnp.logical_not(head0)):
            s = _dg(jnp.where(mask, q, 0.0).astype(BF16), kw, NT)
            s = jnp.where(valid, s, NEG_INF)
            mx = jnp.max(s, axis=-1, keepdims=True)
            p = jnp.exp(s - mx)
            den = jnp.sum(p, axis=-1, keepdims=True)
            o = _dg(p.astype(BF16), vw)
            out = out + jnp.where(mask, o * (1.0 / den), 0.0)
            lse = lse + jnp.where(mask, mx + jnp.log(den), 0.0)
        o_ref[0, rows, :] = out
        l_ref[0, rows, :] = lse


def _dilated_group(q, k, v, gi, dilation):
    B, T, _ = q.shape
    span = dilation * DIL_BLOCK
    pairs = GROUP_WIDTH // PAIR
    cur = pl.BlockSpec((1, span, PAIR), lambda b, p, n: (b, n, gi * pairs + p))
    prev = pl.BlockSpec((1, span, PAIR), lambda b, p, n: (b, jnp.maximum(n - 1, 0), gi * pairs + p))
    out = pl.BlockSpec((1, span, PAIR), lambda b, p, n: (b, n, p))
    sd = jax.ShapeDtypeStruct((B, T, GROUP_WIDTH), F32)
    return pl.pallas_call(
        functools.partial(_dil_kernel, dilation=dilation),
        grid=(B, pairs, T // span),
        in_specs=[cur, cur, prev, cur, prev],
        out_specs=[out, out],
        out_shape=[sd, sd],
        compiler_params=pltpu.CompilerParams(
            dimension_semantics=("parallel", "parallel", "parallel"), vmem_limit_bytes=VMEM_LIMIT),
        name=f"dilated_d{dilation}",
    )(q, k, k, v, v)


def _rope_tables(T):
    inv = ROPE_THETA ** (-jnp.arange(0, HEAD_DIM, 2, dtype=F32) / HEAD_DIM)
    ang = jnp.arange(T, dtype=F32)[:, None] * inv[None, :]
    cos, sin = jnp.cos(ang), jnp.sin(ang)
    cos_t = jnp.tile(cos, (1, LANES // (HEAD_DIM // 2)))
    sin_t = jnp.tile(jnp.concatenate([-sin, sin], axis=1), (1, LANES // HEAD_DIM))
    return cos_t, sin_t


def kernel(x, mem, attn_norm, a_w_in, a_mu, a_w0, a_w2, a_a0, a_a2, a_g2, a_k_k, a_k_a, a_r_k, a_lnx_w, a_lnx_b, a_w_out, kv_norm, kv_w, kv_k_norm, b_w_in, b_q_norm, b_w_out, mem_norm, mem_w_kv, mem_q_norm, mem_k_norm, ffn_norm, ffn_w_up, ffn_conv_w, ffn_conv_b, ffn_w_down):
    B, T, D = x.shape
    depth = attn_norm.shape[0]
    n_a = a_w_in.shape[0]
    seg = jnp.arange(MEM_WIDTH) // HEAD_DIM
    ones_bd = (seg[:, None] == seg[None, :]).astype(BF16)
    kmem, vmem = _mem_kv(mem, mem_norm, mem_w_kv, mem_k_norm, ones_bd)
    cos_t, sin_t = _rope_tables(T)
    k_sh = v_sh = None
    for i in range(depth):
        mqn = jnp.tile(mem_q_norm[i], MEM_HEADS).reshape(1, MEM_WIDTH)
        if i < n_a:
            j = i
            r, lw, k, v, kk, a, g, bon, ymem = _a_in(
                x, attn_norm[i], a_w_in[j], a_mu[j], a_w0[j], a_w2[j], a_a0[j], a_a2[j], a_g2[j],
                a_k_k[j], a_k_a[j], a_r_k[j].reshape(-1), kmem, vmem, i, mqn, ones_bd)
            ymix = _scan(r, lw, k, v, kk, a, g, bon, a_lnx_w[j], a_lnx_b[j])
            x = _a_out(ymix, ymem, x, a_w_out[j])
        else:
            j = i - n_a
            q, k_new, v_new, ymem = _b_in(x, kv_norm, attn_norm[i], kv_w, b_w_in[j], kv_k_norm, b_q_norm[j],
                                          cos_t, sin_t, kmem, vmem, i, mqn, ones_bd)
            if j == 0:
                k_sh, v_sh = k_new, v_new
            outs, lses = [], []
            for gi, (_, dil) in enumerate(DIL_GROUPS):
                o, l = _dilated_group(q, k_sh, v_sh, gi, dil)
                outs.append(o)
                lses.append(l)
            x = _b_out(outs, lses, ymem, x, b_w_out[j])
        x = _ffn(x, ffn_norm[i], ffn_w_up[i], ffn_conv_w[i], ffn_conv_b[i], ffn_w_down[i])
    return x
```

```python
import functools
import math

import jax
import jax.numpy as jnp
from jax import lax
from jax.experimental import pallas as pl
from jax.experimental.pallas import tpu as pltpu

F32 = jnp.float32
BF16 = jnp.bfloat16

D_MODEL = 1024
HEAD_DIM = 64
N_MEM = 256
MEM_HEADS = 4
MEM_WIDTH = MEM_HEADS * HEAD_DIM
RWKV_HEADS = 12
RWKV_WIDTH = RWKV_HEADS * HEAD_DIM
DECAY_LORA = 64
AAA_LORA = 64
GATE_LORA = 128
SHIFT_WIDTH = 3 * RWKV_WIDTH + DECAY_LORA + AAA_LORA + GATE_LORA
A_IN_WIDTH = SHIFT_WIDTH + MEM_WIDTH
DIL_GROUPS = ((128, 1), (512, 4), (2048, 16))
DIL_HEADS_PER_GROUP = 4
GROUP_WIDTH = DIL_HEADS_PER_GROUP * HEAD_DIM
DIL_WIDTH = GROUP_WIDTH * len(DIL_GROUPS)
DIL_BLOCK = 128
D_FF = 2816
ROPE_THETA = 10000.0
RMS_EPS = 1e-6
LNX_EPS = 64e-5
NEG_INF = -1e30

LANES = 128
PAIR = 2 * HEAD_DIM
CHUNK = 64
VMEM_LIMIT = 52 * 1024 * 1024

NN = ((1,), (0,))
NT = ((1,), (1,))
TN = ((0,), (0,))


def _dg(a, b, dims=NN):
    return lax.dot_general(a, b, (dims, ((), ())), preferred_element_type=F32)


def _bdot(a, b, dims=NN):
    return _dg(a.astype(BF16), b.astype(BF16), dims)


def _split2(x):
    hi = x.astype(BF16)
    lo = (x - hi.astype(F32)).astype(BF16)
    return hi, lo


def _dot3(a, b, dims=NN):
    ah, al = _split2(a)
    bh, bl = _split2(b)
    return _dg(ah, bh, dims) + _dg(ah, bl, dims) + _dg(al, bh, dims)


def _seg_sum(x, ones_bd):
    w = ones_bd.shape[0]
    outs = [_dg(x[:, j * w:(j + 1) * w].astype(BF16), ones_bd) for j in range(x.shape[1] // w)]
    return outs[0] if len(outs) == 1 else jnp.concatenate(outs, axis=1)


def _rms_scale(x):
    return lax.rsqrt(jnp.mean(x * x, axis=-1, keepdims=True) + RMS_EPS)


def _sigmoid(z):
    return 1.0 / (1.0 + jnp.exp(-z))


def _head_rms(z, gain_tiled, ones_bd):
    ms = _seg_sum(z * z, ones_bd) * (1.0 / HEAD_DIM)
    return z * lax.rsqrt(ms + RMS_EPS) * gain_tiled


def _mem_attn(qm, kmem, vmem, qn_tiled, ones_bd):
    qh = _head_rms(qm, qn_tiled, ones_bd) * (1.0 / math.sqrt(HEAD_DIM))
    lane = lax.broadcasted_iota(jnp.int32, (1, MEM_WIDTH), 1)
    out = jnp.zeros_like(qm)
    for h in range(MEM_HEADS):
        mask = (lane >= h * HEAD_DIM) & (lane < (h + 1) * HEAD_DIM)
        s = _dg(jnp.where(mask, qh, 0.0).astype(BF16), kmem, NT)
        mx = jnp.max(s, axis=-1, keepdims=True)
        p = jnp.exp(s - mx)
        den = jnp.sum(p, axis=-1, keepdims=True)
        o = _dg(p.astype(BF16), vmem)
        out = out + jnp.where(mask, o * (1.0 / den), 0.0)
    return out


def _shift_rows(u, n, carry_rows):
    rolled = pltpu.roll(u, n, axis=0)
    head = rolled[:8]
    row = lax.broadcasted_iota(jnp.int32, head.shape, 0)
    for i in range(n):
        head = jnp.where(row == i, carry_rows[i], head)
    return jnp.concatenate([head, rolled[8:]], axis=0)


def _memkv_kernel(mem_ref, g_ref, w_ref, kn_ref, ones_ref, k_out, v_out):
    m = mem_ref[0]
    h = m * _rms_scale(m) * g_ref[0]
    kv = _bdot(h, w_ref[0])
    k = _head_rms(kv[:, :MEM_WIDTH], kn_ref[0], ones_ref[...])
    k_out[0, 0] = k.astype(BF16)
    v_out[0, 0] = kv[:, MEM_WIDTH:].astype(BF16)


def _mem_kv(mem, mem_norm, mem_w_kv, mem_k_norm, ones_bd):
    depth = mem_norm.shape[0]
    B = mem.shape[0]
    kn = jnp.tile(mem_k_norm, (1, MEM_HEADS)).reshape(depth, 1, MEM_WIDTH)
    out_sd = jax.ShapeDtypeStruct((depth, B, N_MEM, MEM_WIDTH), BF16)
    return pl.pallas_call(
        _memkv_kernel,
        grid=(depth, B),
        in_specs=[
            pl.BlockSpec((1, N_MEM, D_MODEL), lambda l, b: (b, 0, 0)),
            pl.BlockSpec((1, 1, D_MODEL), lambda l, b: (l, 0, 0)),
            pl.BlockSpec((1, D_MODEL, 2 * MEM_WIDTH), lambda l, b: (l, 0, 0)),
            pl.BlockSpec((1, 1, MEM_WIDTH), lambda l, b: (l, 0, 0)),
            pl.BlockSpec((MEM_WIDTH, MEM_WIDTH), lambda l, b: (0, 0)),
        ],
        out_specs=[pl.BlockSpec((1, 1, N_MEM, MEM_WIDTH), lambda l, b: (l, b, 0, 0))] * 2,
        out_shape=[out_sd, out_sd],
        name="mem_kv",
    )(mem, mem_norm.reshape(depth, 1, D_MODEL), mem_w_kv.astype(BF16), kn, ones_bd)


def _a_in_kernel(x_ref, g_ref, w_ref, mu_ref, w0_ref, w2_ref, a0_ref, a2_ref, g2_ref, kk_ref, ka_ref,
                 rk_ref, kmem_ref, vmem_ref, qn_ref, ones_ref,
                 r_out, lw_out, k_out, v_out, kk_out, a_out, g_out, bon_out, ym_out, carry_ref):
    W = RWKV_WIDTH

    @pl.when(pl.program_id(1) == 0)
    def _():
        carry_ref[...] = jnp.zeros_like(carry_ref)

    ones = ones_ref[...]
    x = x_ref[0]
    h = x * _rms_scale(x) * g_ref[...]
    p = _bdot(h, w_ref[...])
    ps = p[:, :SHIFT_WIDTH]
    prev = _shift_rows(ps, 1, [carry_ref[0:1, :]])
    carry_ref[0:1, :] = ps[ps.shape[0] - 1:, :]
    xs = ps + (prev - ps) * mu_ref[...]
    r = xs[:, 0:W]
    k = xs[:, W:2 * W]
    v = xs[:, 2 * W:3 * W]
    wa = xs[:, 3 * W:3 * W + DECAY_LORA + AAA_LORA]
    gd = xs[:, 3 * W + DECAY_LORA + AAA_LORA:]
    wl = w0_ref[...] + _dot3(jnp.tanh(wa), w2_ref[...])
    z = -wl
    softplus = jnp.maximum(z, 0.0) + jnp.log(1.0 + jnp.exp(-jnp.abs(z)))
    lw_out[0] = -jnp.exp(-softplus - 0.5)
    a = _sigmoid(a0_ref[...] + _bdot(wa, a2_ref[...]))
    g_out[0] = _bdot(_sigmoid(gd), g2_ref[...])
    kkr = k * kk_ref[...]
    n2 = _seg_sum(kkr * kkr, ones)
    kk_out[0] = kkr * lax.rsqrt(jnp.maximum(n2, 1e-24))
    kmod = k * (1.0 + (a - 1.0) * ka_ref[...])
    bon_out[0] = _seg_sum(r * kmod * rk_ref[...], ones) * v
    r_out[0] = r
    k_out[0] = kmod
    v_out[0] = v
    a_out[0] = a
    ym_out[0] = _mem_attn(p[:, SHIFT_WIDTH:], kmem_ref[0, 0], vmem_ref[0, 0], qn_ref[...], ones)


def _a_in(x, g, w_in, mu, w0, w2, a0, a2, g2, k_k, k_a, r_k, kmem, vmem, layer, qn, ones_bd, tm=256):
    B, T, D = x.shape
    W = RWKV_WIDTH
    zpad = jnp.zeros((DECAY_LORA, W), F32)
    w2p = jnp.concatenate([w2, zpad], axis=0)
    a2p = jnp.concatenate([zpad, a2], axis=0).astype(BF16)
    row = lambda a: a.reshape(1, -1)
    const = lambda shape: pl.BlockSpec(shape, lambda b, t: (0,) * len(shape))
    tok = lambda w: pl.BlockSpec((1, tm, w), lambda b, t: (b, t, 0))
    memspec = pl.BlockSpec((1, 1, N_MEM, MEM_WIDTH), lambda b, t: (layer, b, 0, 0))
    sd = lambda w: jax.ShapeDtypeStruct((B, T, w), F32)
    return pl.pallas_call(
        _a_in_kernel,
        grid=(B, T // tm),
        in_specs=[tok(D), const((1, D)), const((D, A_IN_WIDTH)), const((1, SHIFT_WIDTH)),
                  const((1, W)), const((PAIR, W)), const((1, W)), const((PAIR, W)), const((GATE_LORA, W)),
                  const((1, W)), const((1, W)), const((1, W)), memspec, memspec,
                  const((1, MEM_WIDTH)), const((MEM_WIDTH, MEM_WIDTH))],
        out_specs=[tok(W)] * 8 + [tok(MEM_WIDTH)],
        out_shape=[sd(W)] * 8 + [sd(MEM_WIDTH)],
        scratch_shapes=[pltpu.VMEM((8, SHIFT_WIDTH), F32)],
        compiler_params=pltpu.CompilerParams(
            dimension_semantics=("parallel", "arbitrary"), vmem_limit_bytes=VMEM_LIMIT),
        name="a_in",
    )(x, row(g), w_in.astype(BF16), row(mu), row(w0), w2p, row(a0), a2p, g2.astype(BF16),
      row(k_k), row(k_a), row(r_k), kmem, vmem, qn, ones_bd)


def _scan_kernel(r_ref, lw_ref, k_ref, v_ref, kk_ref, a_ref, g_ref, bon_ref, lnw_ref, lnb_ref,
                 o_ref, h_ref, *, n_chunks, n_pairs):
    C = CHUNK

    @pl.when(pl.program_id(2) == 0)
    def _():
        h_ref[...] = jnp.zeros_like(h_ref)

    ri = lax.broadcasted_iota(jnp.int32, (PAIR, PAIR), 0)
    ci = lax.broadcasted_iota(jnp.int32, (PAIR, PAIR), 1)
    head0 = lax.broadcasted_iota(jnp.int32, (1, PAIR), 1) < HEAD_DIM
    strict = ri > ci
    incl = ri >= ci
    same = (ri >= HEAD_DIM) == (ci >= HEAD_DIM)
    eye = (ri == ci).astype(F32)
    ones_bd = same.astype(BF16)
    tril = (lax.broadcasted_iota(jnp.int32, (C, C), 0) >= lax.broadcasted_iota(jnp.int32, (C, C), 1)).astype(BF16)

    def level_mask(b):
        sh = b.bit_length()
        return ((ri & b) != 0) & ((ci & b) == 0) & ((ri >> sh) == (ci >> sh))

    def ms(z):
        return jnp.concatenate([jnp.where(head0, z, 0.0), jnp.where(head0, 0.0, z)], axis=0)

    def fold(z):
        return z[:C] + z[C:]

    items = [(p, c) for p in range(n_pairs) for c in range(n_chunks)]
    n = len(items)

    def tile(ref, it):
        p, c = it
        return ref[0, c * C:(c + 1) * C, p * PAIR:(p + 1) * PAIR]

    rp, vcs, lhs, rhs, bt_t, kt_t, p_tot = [], [], [], [], [], [], []
    for it in items:
        rc, lwc, kc, vc, kkc, ac = (tile(ref, it) for ref in (r_ref, lw_ref, k_ref, v_ref, kk_ref, a_ref))
        l1 = lwc.astype(BF16)
        rem = lwc - l1.astype(F32)
        l2 = rem.astype(BF16)
        l3 = (rem - l2.astype(F32)).astype(BF16)
        cl = _dg(tril, l1) + _dg(tril, l2) + _dg(tril, l3)
        clast = cl[C - 1:C, :]
        p_inv = jnp.exp(-cl)
        p_end = jnp.exp(clast - cl)
        b = kkc * ac
        rpi = rc * jnp.exp(cl)
        rp.append(rpi)
        vcs.append(vc)
        p_tot.append(jnp.exp(clast))
        lhs.append(jnp.concatenate([ms(kkc * jnp.exp(cl - lwc)), ms(rpi)], axis=0).astype(BF16))
        rhs.append(jnp.concatenate([ms(b * p_inv), ms(kc * p_inv)], axis=0).astype(BF16))
        bt_t.append((b * p_end).T.astype(BF16))
        kt_t.append((kc * p_end).T.astype(BF16))
    msv = [ms(vc).astype(BF16) for vc in vcs]

    m = [_dg(lhs[i], rhs[i], NT) for i in range(n)]
    a_ab = [jnp.where(strict, m[i][:PAIR, :PAIR], 0.0) for i in range(n)]
    ak_rk = [jnp.concatenate([jnp.where(strict, m[i][:PAIR, PAIR:], 0.0),
                              jnp.where(incl, m[i][PAIR:, PAIR:], 0.0)], axis=0).astype(BF16) for i in range(n)]
    b_rb = [jnp.where(incl, m[i][PAIR:, :PAIR], 0.0).astype(BF16) for i in range(n)]
    av_y = [_dg(ak_rk[i], msv[i]) for i in range(n)]

    x = [eye - jnp.where(level_mask(1), a_ab[i], 0.0) for i in range(n)]
    bsz = 2
    while bsz < C:
        lm = level_mask(bsz)
        xb = [xi.astype(BF16) for xi in x]
        xe = [_dg(xb[i], jnp.where(lm, a_ab[i], 0.0).astype(BF16)).astype(BF16) for i in range(n)]
        x = [x[i] - _dg(xe[i], xb[i]) for i in range(n)]
        bsz *= 2

    wu = [_dg(x[i].astype(BF16),
              jnp.concatenate([lhs[i][:PAIR], ms(fold(av_y[i][:PAIR])).astype(BF16)], axis=1)) for i in range(n)]
    w = [fold(wu[i][:, :PAIR]) for i in range(n)]
    ut = [fold(wu[i][:, PAIR:]) for i in range(n)]
    bw = [_dg(b_rb[i], jnp.concatenate([ms(w[i]), ms(ut[i])], axis=1).astype(BF16)) for i in range(n)]
    wut = [jnp.concatenate([w[i], ut[i]], axis=1).astype(BF16) for i in range(n)]
    btwu = [_dg(bt_t[i], wut[i]) for i in range(n)]
    ktv = [_dg(kt_t[i], vcs[i].astype(BF16)) for i in range(n)]
    rg = [jnp.concatenate([rp[i] - fold(bw[i][:, :PAIR]),
                           jnp.where(same, eye * p_tot[i] - btwu[i][:, :PAIR], 0.0)], axis=0).astype(BF16)
          for i in range(n)]
    ypp = [fold(av_y[i][PAIR:]) - fold(bw[i][:, PAIR:]) for i in range(n)]
    f_mat = [jnp.where(same, ktv[i] - btwu[i][:, PAIR:], 0.0) for i in range(n)]

    H = [h_ref[p] for p in range(n_pairs)]
    ys = [None] * n
    for c in range(n_chunks):
        for p in range(n_pairs):
            i = p * n_chunks + c
            yh = _dg(rg[i], H[p].astype(BF16))
            ys[i] = yh[:C] + ypp[i]
            H[p] = yh[C:] + f_mat[i]
    for p in range(n_pairs):
        h_ref[p] = H[p]

    for i, it in enumerate(items):
        p, c = it
        y = ys[i]
        mean = _seg_sum(y, ones_bd) * (1.0 / HEAD_DIM)
        d = y - mean
        var = _seg_sum(d * d, ones_bd) * (1.0 / HEAD_DIM)
        lanes = slice(p * PAIR, (p + 1) * PAIR)
        yn = d * lax.rsqrt(var + LNX_EPS) * lnw_ref[:, lanes] + lnb_ref[:, lanes]
        o_ref[0, c * C:(c + 1) * C, lanes] = (yn + tile(bon_ref, it)) * tile(g_ref, it)


def _scan(r, lw, k, v, kk, a, g, bon, lnx_w, lnx_b, tb=256, n_pairs=3):
    B, T, W = r.shape
    wb = n_pairs * PAIR
    tok = pl.BlockSpec((1, tb, wb), lambda b, p, t: (b, t, p))
    par = pl.BlockSpec((1, wb), lambda b, p, t: (0, p))
    return pl.pallas_call(
        functools.partial(_scan_kernel, n_chunks=tb // CHUNK, n_pairs=n_pairs),
        grid=(B, W // wb, T // tb),
        in_specs=[tok] * 8 + [par, par],
        out_specs=tok,
        out_shape=jax.ShapeDtypeStruct((B, T, W), F32),
        scratch_shapes=[pltpu.VMEM((n_pairs, PAIR, PAIR), F32)],
        compiler_params=pltpu.CompilerParams(
            dimension_semantics=("parallel", "parallel", "arbitrary"), vmem_limit_bytes=VMEM_LIMIT),
        name="rwkv_scan",
    )(r, lw, k, v, kk, a, g, bon, lnx_w.reshape(1, W), lnx_b.reshape(1, W))


FFN_COLS = 256


def _a_mix_ffn_kernel(ymix_ref, ymem_ref, x_ref, w1_ref, w2_ref, *rest):
    x1 = x_ref[0] + _bdot(ymix_ref[0], w1_ref[...]) + _bdot(ymem_ref[0], w2_ref[...])
    _ffn_body(x1, *rest)


def _b_mix_ffn_kernel(o1_ref, o2_ref, o3_ref, l1_ref, l2_ref, l3_ref, ymem_ref, x_ref, w1_ref, w2_ref, *rest):
    l1, l2, l3 = l1_ref[0], l2_ref[0], l3_ref[0]
    mx = jnp.maximum(jnp.maximum(l1, l2), l3)
    e1, e2, e3 = jnp.exp(l1 - mx), jnp.exp(l2 - mx), jnp.exp(l3 - mx)
    mix = (e1 * o1_ref[0] + e2 * o2_ref[0] + e3 * o3_ref[0]) * (1.0 / (e1 + e2 + e3))
    x1 = x_ref[0] + _bdot(mix, w1_ref[...]) + _bdot(ymem_ref[0], w2_ref[...])
    _ffn_body(x1, *rest)


def _ffn_body(x, g_ref, wup_ref, cw_ref, cb_ref, wd_ref, o_ref, z_ref, carry_ref):
    @pl.when(pl.program_id(1) == 0)
    def _():
        carry_ref[...] = jnp.zeros_like(carry_ref)

    h = (x * _rms_scale(x) * g_ref[...]).astype(BF16)
    tm = h.shape[0]

    carry = carry_ref[...]
    tails = {}

    def conv(col):
        cols = slice(col, col + FFN_COLS)
        u = _dg(h, wup_ref[:, cols])
        c0 = carry[0:1, cols]
        c1 = carry[1:2, cols]
        tails[col] = u[tm - 2:, :]
        return (cb_ref[:, cols] + cw_ref[0:1, cols] * _shift_rows(u, 2, [c0, c1])
                + cw_ref[1:2, cols] * _shift_rows(u, 1, [c1]) + cw_ref[2:3, cols] * u)

    for j in range(D_FF // FFN_COLS):
        cg = conv(j * FFN_COLS)
        cv = conv(D_FF + j * FFN_COLS)
        z_ref[:, j * FFN_COLS:(j + 1) * FFN_COLS] = (cg * _sigmoid(cg) * cv).astype(BF16)
    o_ref[0] = x + _dg(z_ref[...], wd_ref[...])
    carry_ref[0:2, :] = jnp.concatenate([tails[c] for c in sorted(tails)], axis=1)


def _mix_ffn(mix_kernel, mix_inputs, x, w_out, n_mix, g, w_up, conv_w, conv_b, w_down, tm=512):
    B, T, D = x.shape
    resident = lambda shape: pl.BlockSpec(shape, lambda b, t: (0, 0), pipeline_mode=pl.Buffered(1))
    tok = lambda w: pl.BlockSpec((1, tm, w), lambda b, t: (b, t, 0))
    wb = w_out.astype(BF16)
    return pl.pallas_call(
        mix_kernel,
        grid=(B, T // tm),
        in_specs=[tok(a.shape[-1]) for a in mix_inputs] + [
            tok(D),
            resident((n_mix, D)),
            resident((MEM_WIDTH, D)),
            resident((1, D)),
            resident((D, 2 * D_FF)),
            resident((3, 2 * D_FF)),
            resident((1, 2 * D_FF)),
            resident((D_FF, D)),
        ],
        out_specs=tok(D),
        out_shape=jax.ShapeDtypeStruct((B, T, D), F32),
        scratch_shapes=[pltpu.VMEM((tm, D_FF), BF16), pltpu.VMEM((8, 2 * D_FF), F32)],
        compiler_params=pltpu.CompilerParams(
            dimension_semantics=("parallel", "arbitrary"), vmem_limit_bytes=VMEM_LIMIT),
        name="mix_ffn",
    )(*mix_inputs, x, wb[:n_mix], wb[n_mix:], g.reshape(1, D), w_up.astype(BF16), conv_w,
      conv_b.reshape(1, -1), w_down.astype(BF16))


def _rope(z, cos, sin_signed):
    lane = lax.broadcasted_iota(jnp.int32, (1, LANES), 1)
    first_half = (lane % HEAD_DIM) < HEAD_DIM // 2
    outs = []
    for j in range(z.shape[1] // LANES):
        zs = z[:, j * LANES:(j + 1) * LANES]
        rot = jnp.where(first_half, pltpu.roll(zs, LANES - HEAD_DIM // 2, axis=1),
                        pltpu.roll(zs, HEAD_DIM // 2, axis=1))
        outs.append(zs * cos + rot * sin_signed)
    return jnp.concatenate(outs, axis=1)


def _b_in_kernel(x_ref, gkv_ref, gq_ref, wkv_ref, wq_ref, kn_ref, qn_ref, cos_ref, sin_ref,
                 kmem_ref, vmem_ref, mqn_ref, ones_ref, q_out, k_out, v_out, ym_out):
    ones = ones_ref[...]
    x = x_ref[0]
    xn = x * _rms_scale(x)
    kvp = _bdot(xn * gkv_ref[...], wkv_ref[...])
    p = _bdot(xn * gq_ref[...], wq_ref[...])
    cos, sin = cos_ref[...], sin_ref[...]
    k_out[0] = _rope(_head_rms(kvp[:, :DIL_WIDTH], kn_ref[...], ones), cos, sin)
    v_out[0] = kvp[:, DIL_WIDTH:]
    q = _rope(_head_rms(p[:, :DIL_WIDTH], qn_ref[...], ones), cos, sin)
    q_out[0] = q * (1.0 / math.sqrt(HEAD_DIM))
    ym_out[0] = _mem_attn(p[:, DIL_WIDTH:], kmem_ref[0, 0], vmem_ref[0, 0], mqn_ref[...], ones)


def _b_in(x, g_kv, g_q, kv_w, w_in, k_norm, q_norm, cos, sin, kmem, vmem, layer, mqn, ones_bd, tm=256):
    B, T, D = x.shape
    const = lambda shape: pl.BlockSpec(shape, lambda b, t: (0,) * len(shape))
    tok = lambda w: pl.BlockSpec((1, tm, w), lambda b, t: (b, t, 0))
    tab = pl.BlockSpec((tm, LANES), lambda b, t: (t, 0))
    memspec = pl.BlockSpec((1, 1, N_MEM, MEM_WIDTH), lambda b, t: (layer, b, 0, 0))
    sd = lambda w: jax.ShapeDtypeStruct((B, T, w), F32)
    tile_heads = lambda gain: jnp.tile(gain, DIL_WIDTH // HEAD_DIM).reshape(1, DIL_WIDTH)
    return pl.pallas_call(
        _b_in_kernel,
        grid=(B, T // tm),
        in_specs=[tok(D), const((1, D)), const((1, D)), const((D, 2 * DIL_WIDTH)),
                  const((D, DIL_WIDTH + MEM_WIDTH)), const((1, DIL_WIDTH)), const((1, DIL_WIDTH)),
                  tab, tab, memspec, memspec, const((1, MEM_WIDTH)), const((MEM_WIDTH, MEM_WIDTH))],
        out_specs=[tok(DIL_WIDTH)] * 3 + [tok(MEM_WIDTH)],
        out_shape=[sd(DIL_WIDTH)] * 3 + [sd(MEM_WIDTH)],
        compiler_params=pltpu.CompilerParams(
            dimension_semantics=("parallel", "parallel"), vmem_limit_bytes=VMEM_LIMIT),
        name="b_in",
    )(x, g_kv.reshape(1, D), g_q.reshape(1, D), kv_w.astype(BF16), w_in.astype(BF16),
      tile_heads(k_norm), tile_heads(q_norm), cos, sin, kmem, vmem, mqn, ones_bd)


DIL_TILE_ROWS = 1024


def _dil_kernel(q_ref, kc_ref, kp_ref, vc_ref, vp_ref, o_ref, l_ref, *, dilation, spans):
    n = pl.program_id(2)
    blk = DIL_BLOCK
    span = dilation * blk
    i = lax.broadcasted_iota(jnp.int32, (blk, 2 * blk), 0)
    jj = lax.broadcasted_iota(jnp.int32, (blk, 2 * blk), 1)
    dist = blk + i - jj
    band = (dist >= 0) & (dist <= blk)
    band_first = band & ((n > 0) | (jj >= blk))
    head0 = lax.broadcasted_iota(jnp.int32, (1, PAIR), 1) < HEAD_DIM

    def sub(start):
        return pl.ds(start, blk, stride=dilation) if dilation > 1 else pl.ds(start, blk)

    for s in range(spans):
        for r in range(dilation):
            rows = sub(s * span + r)
            if s == 0:
                kprev, vprev, valid = kp_ref[0, sub(r), :], vp_ref[0, sub(r), :], band_first
            else:
                prow = sub((s - 1) * span + r)
                kprev, vprev, valid = kc_ref[0, prow, :], vc_ref[0, prow, :], band
            q = q_ref[0, rows, :]
            kw = jnp.concatenate([kprev, kc_ref[0, rows, :]], axis=0).astype(BF16)
            vw = jnp.concatenate([vprev, vc_ref[0, rows, :]], axis=0).astype(BF16)
            out = jnp.zeros((blk, PAIR), F32)
            lse = jnp.zeros((blk, PAIR), F32)
            for mask in (head0, jnp.logical_not(head0)):
                sc = _dg(jnp.where(mask, q, 0.0).astype(BF16), kw, NT)
                sc = jnp.where(valid, sc, NEG_INF)
                mx = jnp.max(sc, axis=-1, keepdims=True)
                p = jnp.exp(sc - mx)
                den = jnp.sum(p, axis=-1, keepdims=True)
                o = _dg(p.astype(BF16), vw)
                out = out + jnp.where(mask, o * (1.0 / den), 0.0)
                lse = lse + jnp.where(mask, mx + jnp.log(den), 0.0)
            o_ref[0, rows, :] = out
            l_ref[0, rows, :] = lse


def _dilated_group(q, k, v, gi, dilation):
    B, T, _ = q.shape
    span = dilation * DIL_BLOCK
    spans = max(1, DIL_TILE_ROWS // span)
    rows = spans * span
    pairs = GROUP_WIDTH // PAIR
    cur = pl.BlockSpec((1, rows, PAIR), lambda b, p, n: (b, n, gi * pairs + p))
    prev = pl.BlockSpec((1, span, PAIR), lambda b, p, n: (b, jnp.maximum(n * spans - 1, 0), gi * pairs + p))
    out = pl.BlockSpec((1, rows, PAIR), lambda b, p, n: (b, n, p))
    sd = jax.ShapeDtypeStruct((B, T, GROUP_WIDTH), F32)
    return pl.pallas_call(
        functools.partial(_dil_kernel, dilation=dilation, spans=spans),
        grid=(B, pairs, T // rows),
        in_specs=[cur, cur, prev, cur, prev],
        out_specs=[out, out],
        out_shape=[sd, sd],
        compiler_params=pltpu.CompilerParams(
            dimension_semantics=("parallel", "parallel", "parallel"), vmem_limit_bytes=VMEM_LIMIT),
        name=f"dilated_d{dilation}",
    )(q, k, k, v, v)


def _rope_tables(T):
    inv = ROPE_THETA ** (-jnp.arange(0, HEAD_DIM, 2, dtype=F32) / HEAD_DIM)
    ang = jnp.arange(T, dtype=F32)[:, None] * inv[None, :]
    cos, sin = jnp.cos(ang), jnp.sin(ang)
    cos_t = jnp.tile(cos, (1, LANES // (HEAD_DIM // 2)))
    sin_t = jnp.tile(jnp.concatenate([-sin, sin], axis=1), (1, LANES // HEAD_DIM))
    return cos_t, sin_t


def kernel(x, mem, attn_norm, a_w_in, a_mu, a_w0, a_w2, a_a0, a_a2, a_g2, a_k_k, a_k_a, a_r_k, a_lnx_w, a_lnx_b, a_w_out, kv_norm, kv_w, kv_k_norm, b_w_in, b_q_norm, b_w_out, mem_norm, mem_w_kv, mem_q_norm, mem_k_norm, ffn_norm, ffn_w_up, ffn_conv_w, ffn_conv_b, ffn_w_down):
    B, T, D = x.shape
    depth = attn_norm.shape[0]
    n_a = a_w_in.shape[0]
    seg = jnp.arange(MEM_WIDTH) // HEAD_DIM
    ones_bd = (seg[:, None] == seg[None, :]).astype(BF16)
    kmem, vmem = _mem_kv(mem, mem_norm, mem_w_kv, mem_k_norm, ones_bd)
    cos_t, sin_t = _rope_tables(T)
    k_sh = v_sh = None
    for i in range(depth):
        mqn = jnp.tile(mem_q_norm[i], MEM_HEADS).reshape(1, MEM_WIDTH)
        if i < n_a:
            j = i
            r, lw, k, v, kk, a, g, bon, ymem = _a_in(
                x, attn_norm[i], a_w_in[j], a_mu[j], a_w0[j], a_w2[j], a_a0[j], a_a2[j], a_g2[j],
                a_k_k[j], a_k_a[j], a_r_k[j].reshape(-1), kmem, vmem, i, mqn, ones_bd)
            ymix = _scan(r, lw, k, v, kk, a, g, bon, a_lnx_w[j], a_lnx_b[j])
            mix_kernel, mix_inputs, w_out, n_mix = _a_mix_ffn_kernel, [ymix, ymem], a_w_out[j], RWKV_WIDTH
        else:
            j = i - n_a
            q, k_new, v_new, ymem = _b_in(x, kv_norm, attn_norm[i], kv_w, b_w_in[j], kv_k_norm, b_q_norm[j],
                                          cos_t, sin_t, kmem, vmem, i, mqn, ones_bd)
            if j == 0:
                k_sh, v_sh = k_new, v_new
            outs, lses = [], []
            for gi, (_, dil) in enumerate(DIL_GROUPS):
                o, l = _dilated_group(q, k_sh, v_sh, gi, dil)
                outs.append(o)
                lses.append(l)
            mix_kernel, mix_inputs, w_out, n_mix = _b_mix_ffn_kernel, outs + lses + [ymem], b_w_out[j], GROUP_WIDTH
        x = _mix_ffn(mix_kernel, mix_inputs, x, w_out, n_mix,
                     ffn_norm[i], ffn_w_up[i], ffn_conv_w[i], ffn_conv_b[i], ffn_w_down[i])
    return x
```

```python
import functools
import math

import jax
import jax.numpy as jnp
from jax import lax
from jax.experimental import pallas as pl
from jax.experimental.pallas import tpu as pltpu

F32 = jnp.float32
BF16 = jnp.bfloat16

D_MODEL = 1024
HEAD_DIM = 64
N_MEM = 256
MEM_HEADS = 4
MEM_WIDTH = MEM_HEADS * HEAD_DIM
RWKV_HEADS = 12
RWKV_WIDTH = RWKV_HEADS * HEAD_DIM
DECAY_LORA = 64
AAA_LORA = 64
GATE_LORA = 128
SHIFT_WIDTH = 3 * RWKV_WIDTH + DECAY_LORA + AAA_LORA + GATE_LORA
A_IN_WIDTH = SHIFT_WIDTH + MEM_WIDTH
DIL_GROUPS = ((128, 1), (512, 4), (2048, 16))
DIL_HEADS_PER_GROUP = 4
GROUP_WIDTH = DIL_HEADS_PER_GROUP * HEAD_DIM
DIL_WIDTH = GROUP_WIDTH * len(DIL_GROUPS)
DIL_BLOCK = 128
D_FF = 2816
ROPE_THETA = 10000.0
RMS_EPS = 1e-6
LNX_EPS = 64e-5
NEG_INF = -1e30

LANES = 128
PAIR = 2 * HEAD_DIM
CHUNK = 64
SUB_ROWS = 256
VMEM_LIMIT = 52 * 1024 * 1024

NN = ((1,), (0,))
NT = ((1,), (1,))
TN = ((0,), (0,))


def _dg(a, b, dims=NN):
    return lax.dot_general(a, b, (dims, ((), ())), preferred_element_type=F32)


def _bdot(a, b, dims=NN):
    return _dg(a.astype(BF16), b.astype(BF16), dims)


def _split2(x):
    hi = x.astype(BF16)
    lo = (x - hi.astype(F32)).astype(BF16)
    return hi, lo


def _dot3(a, b, dims=NN):
    ah, al = _split2(a)
    bh, bl = _split2(b)
    return _dg(ah, bh, dims) + _dg(ah, bl, dims) + _dg(al, bh, dims)


def _seg_sum(x, ones_bd):
    w = ones_bd.shape[0]
    outs = [_dg(x[:, j * w:(j + 1) * w].astype(BF16), ones_bd) for j in range(x.shape[1] // w)]
    return outs[0] if len(outs) == 1 else jnp.concatenate(outs, axis=1)


def _rms_scale(x):
    return lax.rsqrt(jnp.mean(x * x, axis=-1, keepdims=True) + RMS_EPS)


def _sigmoid(z):
    return 1.0 / (1.0 + jnp.exp(-z))


def _head_rms(z, gain_tiled, ones_bd):
    ms = _seg_sum(z * z, ones_bd) * (1.0 / HEAD_DIM)
    return z * lax.rsqrt(ms + RMS_EPS) * gain_tiled


def _mem_attn(qm, kmem, vmem, qn_tiled, ones_bd):
    qh = _head_rms(qm, qn_tiled, ones_bd) * (1.0 / math.sqrt(HEAD_DIM))
    lane = lax.broadcasted_iota(jnp.int32, (1, MEM_WIDTH), 1)
    out = jnp.zeros_like(qm)
    for h in range(MEM_HEADS):
        mask = (lane >= h * HEAD_DIM) & (lane < (h + 1) * HEAD_DIM)
        s = _dg(jnp.where(mask, qh, 0.0).astype(BF16), kmem, NT)
        mx = jnp.max(s, axis=-1, keepdims=True)
        p = jnp.exp(s - mx)
        den = jnp.sum(p, axis=-1, keepdims=True)
        o = _dg(p.astype(BF16), vmem)
        out = out + jnp.where(mask, o * (1.0 / den), 0.0)
    return out


def _shift_rows(u, n, carry_rows):
    rolled = pltpu.roll(u, n, axis=0)
    head = rolled[:8]
    row = lax.broadcasted_iota(jnp.int32, head.shape, 0)
    for i in range(n):
        head = jnp.where(row == i, carry_rows[i], head)
    return jnp.concatenate([head, rolled[8:]], axis=0)


def _memkv_kernel(mem_ref, g_ref, w_ref, kn_ref, ones_ref, k_out, v_out):
    m = mem_ref[0]
    h = m * _rms_scale(m) * g_ref[0]
    kv = _bdot(h, w_ref[0])
    k = _head_rms(kv[:, :MEM_WIDTH], kn_ref[0], ones_ref[...])
    k_out[0, 0] = k.astype(BF16)
    v_out[0, 0] = kv[:, MEM_WIDTH:].astype(BF16)


def _mem_kv(mem, mem_norm, mem_w_kv, mem_k_norm, ones_bd):
    depth = mem_norm.shape[0]
    B = mem.shape[0]
    kn = jnp.tile(mem_k_norm, (1, MEM_HEADS)).reshape(depth, 1, MEM_WIDTH)
    out_sd = jax.ShapeDtypeStruct((depth, B, N_MEM, MEM_WIDTH), BF16)
    return pl.pallas_call(
        _memkv_kernel,
        grid=(depth, B),
        in_specs=[
            pl.BlockSpec((1, N_MEM, D_MODEL), lambda l, b: (b, 0, 0)),
            pl.BlockSpec((1, 1, D_MODEL), lambda l, b: (l, 0, 0)),
            pl.BlockSpec((1, D_MODEL, 2 * MEM_WIDTH), lambda l, b: (l, 0, 0)),
            pl.BlockSpec((1, 1, MEM_WIDTH), lambda l, b: (l, 0, 0)),
            pl.BlockSpec((MEM_WIDTH, MEM_WIDTH), lambda l, b: (0, 0)),
        ],
        out_specs=[pl.BlockSpec((1, 1, N_MEM, MEM_WIDTH), lambda l, b: (l, b, 0, 0))] * 2,
        out_shape=[out_sd, out_sd],
        name="mem_kv",
    )(mem, mem_norm.reshape(depth, 1, D_MODEL), mem_w_kv.astype(BF16), kn, ones_bd)


def _a_in_kernel(x_ref, g_ref, w_ref, mu_ref, w0_ref, w2_ref, a0_ref, a2_ref, g2_ref, kk_ref, ka_ref,
                 rk_ref, kmem_ref, vmem_ref, qn_ref, ones_ref,
                 r_out, lw_out, k_out, v_out, kk_out, a_out, g_out, bon_out, ym_out, carry_ref):
    W = RWKV_WIDTH

    @pl.when(pl.program_id(1) == 0)
    def _():
        carry_ref[...] = jnp.zeros_like(carry_ref)

    ones = ones_ref[...]
    carry_row = carry_ref[0:1, :]
    n_sub = x_ref.shape[1] // SUB_ROWS
    proj = []
    for s in range(n_sub):
        x = x_ref[0, s * SUB_ROWS:(s + 1) * SUB_ROWS, :]
        proj.append(_bdot(x * _rms_scale(x) * g_ref[...], w_ref[...]))
    for s in range(n_sub):
        rows = slice(s * SUB_ROWS, (s + 1) * SUB_ROWS)
        p = proj[s]
        ps = p[:, :SHIFT_WIDTH]
        prev = _shift_rows(ps, 1, [carry_row])
        carry_row = ps[SUB_ROWS - 1:, :]
        xs = ps + (prev - ps) * mu_ref[...]
        r = xs[:, 0:W]
        k = xs[:, W:2 * W]
        v = xs[:, 2 * W:3 * W]
        wa = xs[:, 3 * W:3 * W + DECAY_LORA + AAA_LORA]
        gd = xs[:, 3 * W + DECAY_LORA + AAA_LORA:]
        wl = w0_ref[...] + _dot3(jnp.tanh(wa), w2_ref[...])
        z = -wl
        softplus = jnp.maximum(z, 0.0) + jnp.log(1.0 + jnp.exp(-jnp.abs(z)))
        lw_out[0, rows, :] = -jnp.exp(-softplus - 0.5)
        a = _sigmoid(a0_ref[...] + _bdot(wa, a2_ref[...]))
        g_out[0, rows, :] = _bdot(_sigmoid(gd), g2_ref[...])
        kkr = k * kk_ref[...]
        n2 = _seg_sum(kkr * kkr, ones)
        kk_out[0, rows, :] = kkr * lax.rsqrt(jnp.maximum(n2, 1e-24))
        kmod = k * (1.0 + (a - 1.0) * ka_ref[...])
        bon_out[0, rows, :] = _seg_sum(r * kmod * rk_ref[...], ones) * v
        r_out[0, rows, :] = r
        k_out[0, rows, :] = kmod
        v_out[0, rows, :] = v
        a_out[0, rows, :] = a
        ym_out[0, rows, :] = _mem_attn(p[:, SHIFT_WIDTH:], kmem_ref[0, 0], vmem_ref[0, 0], qn_ref[...], ones)
    carry_ref[0:1, :] = carry_row


def _a_in(x, g, w_in, mu, w0, w2, a0, a2, g2, k_k, k_a, r_k, kmem, vmem, layer, qn, ones_bd, tm=512):
    B, T, D = x.shape
    W = RWKV_WIDTH
    zpad = jnp.zeros((DECAY_LORA, W), F32)
    w2p = jnp.concatenate([w2, zpad], axis=0)
    a2p = jnp.concatenate([zpad, a2], axis=0).astype(BF16)
    row = lambda a: a.reshape(1, -1)
    const = lambda shape: pl.BlockSpec(shape, lambda b, t: (0,) * len(shape))
    tok = lambda w: pl.BlockSpec((1, tm, w), lambda b, t: (b, t, 0))
    memspec = pl.BlockSpec((1, 1, N_MEM, MEM_WIDTH), lambda b, t: (layer, b, 0, 0))
    sd = lambda w: jax.ShapeDtypeStruct((B, T, w), F32)
    return pl.pallas_call(
        _a_in_kernel,
        grid=(B, T // tm),
        in_specs=[tok(D), const((1, D)), const((D, A_IN_WIDTH)), const((1, SHIFT_WIDTH)),
                  const((1, W)), const((PAIR, W)), const((1, W)), const((PAIR, W)), const((GATE_LORA, W)),
                  const((1, W)), const((1, W)), const((1, W)), memspec, memspec,
                  const((1, MEM_WIDTH)), const((MEM_WIDTH, MEM_WIDTH))],
        out_specs=[tok(W)] * 8 + [tok(MEM_WIDTH)],
        out_shape=[sd(W)] * 8 + [sd(MEM_WIDTH)],
        scratch_shapes=[pltpu.VMEM((8, SHIFT_WIDTH), F32)],
        compiler_params=pltpu.CompilerParams(
            dimension_semantics=("parallel", "arbitrary"), vmem_limit_bytes=VMEM_LIMIT),
        name="a_in",
    )(x, row(g), w_in.astype(BF16), row(mu), row(w0), w2p, row(a0), a2p, g2.astype(BF16),
      row(k_k), row(k_a), row(r_k), kmem, vmem, qn, ones_bd)


def _scan_kernel(r_ref, lw_ref, k_ref, v_ref, kk_ref, a_ref, g_ref, bon_ref, lnw_ref, lnb_ref,
                 o_ref, h_ref, *, n_chunks, n_pairs):
    C = CHUNK

    @pl.when(pl.program_id(2) == 0)
    def _():
        h_ref[...] = jnp.zeros_like(h_ref)

    ri = lax.broadcasted_iota(jnp.int32, (PAIR, PAIR), 0)
    ci = lax.broadcasted_iota(jnp.int32, (PAIR, PAIR), 1)
    head0 = lax.broadcasted_iota(jnp.int32, (1, PAIR), 1) < HEAD_DIM
    strict = ri > ci
    incl = ri >= ci
    same = (ri >= HEAD_DIM) == (ci >= HEAD_DIM)
    eye = (ri == ci).astype(F32)
    row_c = lax.broadcasted_iota(jnp.int32, (C, PAIR), 0)

    def head_sum(z):
        s0 = jnp.sum(jnp.where(head0, z, 0.0), axis=-1, keepdims=True)
        s1 = jnp.sum(jnp.where(head0, 0.0, z), axis=-1, keepdims=True)
        return jnp.where(head0, s0, s1)

    def level_mask(b):
        sh = b.bit_length()
        return ((ri & b) != 0) & ((ci & b) == 0) & ((ri >> sh) == (ci >> sh))

    def ms(z):
        return jnp.concatenate([jnp.where(head0, z, 0.0), jnp.where(head0, 0.0, z)], axis=0)

    def fold(z):
        return z[:C] + z[C:]

    items = [(p, c) for p in range(n_pairs) for c in range(n_chunks)]
    n = len(items)

    def tile(ref, it):
        p, c = it
        return ref[0, c * C:(c + 1) * C, p * PAIR:(p + 1) * PAIR]

    rp, vcs, lhs, rhs, bt_t, kt_t, p_tot = [], [], [], [], [], [], []
    for it in items:
        rc, lwc, kc, vc, kkc, ac = (tile(ref, it) for ref in (r_ref, lw_ref, k_ref, v_ref, kk_ref, a_ref))
        cl = lwc
        step = 1
        while step < C:
            cl = cl + jnp.where(row_c >= step, pltpu.roll(cl, step, axis=0), 0.0)
            step *= 2
        clast = cl[C - 1:C, :]
        p_inv = jnp.exp(-cl)
        p_end = jnp.exp(clast - cl)
        b = kkc * ac
        rpi = rc * jnp.exp(cl)
        rp.append(rpi)
        vcs.append(vc)
        p_tot.append(jnp.exp(clast))
        lhs.append(jnp.concatenate([ms(kkc * jnp.exp(cl - lwc)), ms(rpi)], axis=0).astype(BF16))
        rhs.append(jnp.concatenate([ms(b * p_inv), ms(kc * p_inv)], axis=0).astype(BF16))
        bt_t.append((b * p_end).T.astype(BF16))
        kt_t.append((kc * p_end).T.astype(BF16))
    msv = [ms(vc).astype(BF16) for vc in vcs]

    m = [_dg(lhs[i], rhs[i], NT) for i in range(n)]
    a_ab = [jnp.where(strict, m[i][:PAIR, :PAIR], 0.0) for i in range(n)]
    ak_rk = [jnp.concatenate([jnp.where(strict, m[i][:PAIR, PAIR:], 0.0),
                              jnp.where(incl, m[i][PAIR:, PAIR:], 0.0)], axis=0).astype(BF16) for i in range(n)]
    b_rb = [jnp.where(incl, m[i][PAIR:, :PAIR], 0.0).astype(BF16) for i in range(n)]
    av_y = [_dg(ak_rk[i], msv[i]) for i in range(n)]

    x = [eye - jnp.where(level_mask(1), a_ab[i], 0.0) for i in range(n)]
    bsz = 2
    while bsz < C:
        lm = level_mask(bsz)
        xb = [xi.astype(BF16) for xi in x]
        xe = [_dg(xb[i], jnp.where(lm, a_ab[i], 0.0).astype(BF16)).astype(BF16) for i in range(n)]
        x = [x[i] - _dg(xe[i], xb[i]) for i in range(n)]
        bsz *= 2

    wu = [_dg(x[i].astype(BF16),
              jnp.concatenate([lhs[i][:PAIR], ms(fold(av_y[i][:PAIR])).astype(BF16)], axis=1)) for i in range(n)]
    w = [fold(wu[i][:, :PAIR]) for i in range(n)]
    ut = [fold(wu[i][:, PAIR:]) for i in range(n)]
    bw = [_dg(b_rb[i], jnp.concatenate([ms(w[i]), ms(ut[i])], axis=1).astype(BF16)) for i in range(n)]
    wut = [jnp.concatenate([w[i], ut[i]], axis=1).astype(BF16) for i in range(n)]
    btwu = [_dg(bt_t[i], wut[i]) for i in range(n)]
    ktv = [_dg(kt_t[i], vcs[i].astype(BF16)) for i in range(n)]
    rg = [jnp.concatenate([rp[i] - fold(bw[i][:, :PAIR]),
                           jnp.where(same, eye * p_tot[i] - btwu[i][:, :PAIR], 0.0)], axis=0).astype(BF16)
          for i in range(n)]
    ypp = [fold(av_y[i][PAIR:]) - fold(bw[i][:, PAIR:]) for i in range(n)]
    f_mat = [jnp.where(same, ktv[i] - btwu[i][:, PAIR:], 0.0) for i in range(n)]

    H = [h_ref[p] for p in range(n_pairs)]
    ys = [None] * n
    for c in range(n_chunks):
        for p in range(n_pairs):
            i = p * n_chunks + c
            yh = _dg(rg[i], H[p].astype(BF16))
            ys[i] = yh[:C] + ypp[i]
            H[p] = yh[C:] + f_mat[i]
    for p in range(n_pairs):
        h_ref[p] = H[p]

    for i, it in enumerate(items):
        p, c = it
        y = ys[i]
        mean = head_sum(y) * (1.0 / HEAD_DIM)
        d = y - mean
        var = head_sum(d * d) * (1.0 / HEAD_DIM)
        lanes = slice(p * PAIR, (p + 1) * PAIR)
        yn = d * lax.rsqrt(var + LNX_EPS) * lnw_ref[:, lanes] + lnb_ref[:, lanes]
        o_ref[0, c * C:(c + 1) * C, lanes] = (yn + tile(bon_ref, it)) * tile(g_ref, it)


def _scan(r, lw, k, v, kk, a, g, bon, lnx_w, lnx_b, tb=512, n_pairs=3):
    B, T, W = r.shape
    wb = n_pairs * PAIR
    tok = pl.BlockSpec((1, tb, wb), lambda b, p, t: (b, t, p))
    par = pl.BlockSpec((1, wb), lambda b, p, t: (0, p))
    return pl.pallas_call(
        functools.partial(_scan_kernel, n_chunks=tb // CHUNK, n_pairs=n_pairs),
        grid=(B, W // wb, T // tb),
        in_specs=[tok] * 8 + [par, par],
        out_specs=tok,
        out_shape=jax.ShapeDtypeStruct((B, T, W), F32),
        scratch_shapes=[pltpu.VMEM((n_pairs, PAIR, PAIR), F32)],
        compiler_params=pltpu.CompilerParams(
            dimension_semantics=("parallel", "parallel", "arbitrary"), vmem_limit_bytes=VMEM_LIMIT),
        name="rwkv_scan",
    )(r, lw, k, v, kk, a, g, bon, lnx_w.reshape(1, W), lnx_b.reshape(1, W))


FFN_COLS = 256


def _a_mix_ffn_kernel(ymix_ref, ymem_ref, x_ref, w1_ref, w2_ref, *rest):
    x1 = x_ref[0] + _bdot(ymix_ref[0], w1_ref[...]) + _bdot(ymem_ref[0], w2_ref[...])
    _ffn_body(x1, *rest)


def _b_mix_ffn_kernel(o1_ref, o2_ref, o3_ref, l1_ref, l2_ref, l3_ref, ymem_ref, x_ref, w1_ref, w2_ref, *rest):
    l1, l2, l3 = l1_ref[0], l2_ref[0], l3_ref[0]
    mx = jnp.maximum(jnp.maximum(l1, l2), l3)
    e1, e2, e3 = jnp.exp(l1 - mx), jnp.exp(l2 - mx), jnp.exp(l3 - mx)
    mix = (e1 * o1_ref[0] + e2 * o2_ref[0] + e3 * o3_ref[0]) * (1.0 / (e1 + e2 + e3))
    x1 = x_ref[0] + _bdot(mix, w1_ref[...]) + _bdot(ymem_ref[0], w2_ref[...])
    _ffn_body(x1, *rest)


def _ffn_body(x, g_ref, wup_ref, cw_ref, cb_ref, wd_ref, o_ref, z_ref, carry_ref):
    @pl.when(pl.program_id(1) == 0)
    def _():
        carry_ref[...] = jnp.zeros_like(carry_ref)

    h = (x * _rms_scale(x) * g_ref[...]).astype(BF16)
    tm = h.shape[0]

    carry = carry_ref[...]
    tails = {}

    def conv(col):
        cols = slice(col, col + FFN_COLS)
        u = _dg(h, wup_ref[:, cols])
        c0 = carry[0:1, cols]
        c1 = carry[1:2, cols]
        tails[col] = u[tm - 2:, :]
        return (cb_ref[:, cols] + cw_ref[0:1, cols] * _shift_rows(u, 2, [c0, c1])
                + cw_ref[1:2, cols] * _shift_rows(u, 1, [c1]) + cw_ref[2:3, cols] * u)

    for j in range(D_FF // FFN_COLS):
        cg = conv(j * FFN_COLS)
        cv = conv(D_FF + j * FFN_COLS)
        z_ref[:, j * FFN_COLS:(j + 1) * FFN_COLS] = (cg * _sigmoid(cg) * cv).astype(BF16)
    o_ref[0] = x + _dg(z_ref[...], wd_ref[...])
    carry_ref[0:2, :] = jnp.concatenate([tails[c] for c in sorted(tails)], axis=1)


def _mix_ffn(mix_kernel, mix_inputs, x, w_out, n_mix, g, w_up, conv_w, conv_b, w_down, tm=512):
    B, T, D = x.shape
    resident = lambda shape: pl.BlockSpec(shape, lambda b, t: (0, 0), pipeline_mode=pl.Buffered(1))
    tok = lambda w: pl.BlockSpec((1, tm, w), lambda b, t: (b, t, 0))
    wb = w_out.astype(BF16)
    return pl.pallas_call(
        mix_kernel,
        grid=(B, T // tm),
        in_specs=[tok(a.shape[-1]) for a in mix_inputs] + [
            tok(D),
            resident((n_mix, D)),
            resident((MEM_WIDTH, D)),
            resident((1, D)),
            resident((D, 2 * D_FF)),
            resident((3, 2 * D_FF)),
            resident((1, 2 * D_FF)),
            resident((D_FF, D)),
        ],
        out_specs=tok(D),
        out_shape=jax.ShapeDtypeStruct((B, T, D), F32),
        scratch_shapes=[pltpu.VMEM((tm, D_FF), BF16), pltpu.VMEM((8, 2 * D_FF), F32)],
        compiler_params=pltpu.CompilerParams(
            dimension_semantics=("parallel", "arbitrary"), vmem_limit_bytes=VMEM_LIMIT),
        name="mix_ffn",
    )(*mix_inputs, x, wb[:n_mix], wb[n_mix:], g.reshape(1, D), w_up.astype(BF16), conv_w,
      conv_b.reshape(1, -1), w_down.astype(BF16))


def _rope(z, cos, sin_signed):
    lane = lax.broadcasted_iota(jnp.int32, (1, LANES), 1)
    first_half = (lane % HEAD_DIM) < HEAD_DIM // 2
    outs = []
    for j in range(z.shape[1] // LANES):
        zs = z[:, j * LANES:(j + 1) * LANES]
        rot = jnp.where(first_half, pltpu.roll(zs, LANES - HEAD_DIM // 2, axis=1),
                        pltpu.roll(zs, HEAD_DIM // 2, axis=1))
        outs.append(zs * cos + rot * sin_signed)
    return jnp.concatenate(outs, axis=1)


def _b_in_kernel(x_ref, gkv_ref, gq_ref, wkv_ref, wq_ref, kn_ref, qn_ref, cos_ref, sin_ref,
                 kmem_ref, vmem_ref, mqn_ref, ones_ref, q_out, k_out, v_out, ym_out):
    ones = ones_ref[...]

    n_sub = x_ref.shape[1] // SUB_ROWS
    proj = []
    for s in range(n_sub):
        x = x_ref[0, s * SUB_ROWS:(s + 1) * SUB_ROWS, :]
        xn = x * _rms_scale(x)
        proj.append((_bdot(xn * gkv_ref[...], wkv_ref[...]),
                     _bdot(xn * gq_ref[...], wq_ref[...])))
    for s in range(n_sub):
        rows = slice(s * SUB_ROWS, (s + 1) * SUB_ROWS)
        kvp, p = proj[s]
        cos, sin = cos_ref[rows, :], sin_ref[rows, :]
        k_out[0, rows, :] = _rope(_head_rms(kvp[:, :DIL_WIDTH], kn_ref[...], ones), cos, sin)
        v_out[0, rows, :] = kvp[:, DIL_WIDTH:]
        q = _rope(_head_rms(p[:, :DIL_WIDTH], qn_ref[...], ones), cos, sin)
        q_out[0, rows, :] = q * (1.0 / math.sqrt(HEAD_DIM))
        ym_out[0, rows, :] = _mem_attn(p[:, DIL_WIDTH:], kmem_ref[0, 0], vmem_ref[0, 0], mqn_ref[...], ones)


def _b_in(x, g_kv, g_q, kv_w, w_in, k_norm, q_norm, cos, sin, kmem, vmem, layer, mqn, ones_bd, tm=512):
    B, T, D = x.shape
    const = lambda shape: pl.BlockSpec(shape, lambda b, t: (0,) * len(shape))
    tok = lambda w: pl.BlockSpec((1, tm, w), lambda b, t: (b, t, 0))
    tab = pl.BlockSpec((tm, LANES), lambda b, t: (t, 0))
    memspec = pl.BlockSpec((1, 1, N_MEM, MEM_WIDTH), lambda b, t: (layer, b, 0, 0))
    sd = lambda w: jax.ShapeDtypeStruct((B, T, w), F32)
    tile_heads = lambda gain: jnp.tile(gain, DIL_WIDTH // HEAD_DIM).reshape(1, DIL_WIDTH)
    return pl.pallas_call(
        _b_in_kernel,
        grid=(B, T // tm),
        in_specs=[tok(D), const((1, D)), const((1, D)), const((D, 2 * DIL_WIDTH)),
                  const((D, DIL_WIDTH + MEM_WIDTH)), const((1, DIL_WIDTH)), const((1, DIL_WIDTH)),
                  tab, tab, memspec, memspec, const((1, MEM_WIDTH)), const((MEM_WIDTH, MEM_WIDTH))],
        out_specs=[tok(DIL_WIDTH)] * 3 + [tok(MEM_WIDTH)],
        out_shape=[sd(DIL_WIDTH)] * 3 + [sd(MEM_WIDTH)],
        compiler_params=pltpu.CompilerParams(
            dimension_semantics=("parallel", "parallel"), vmem_limit_bytes=VMEM_LIMIT),
        name="b_in",
    )(x, g_kv.reshape(1, D), g_q.reshape(1, D), kv_w.astype(BF16), w_in.astype(BF16),
      tile_heads(k_norm), tile_heads(q_norm), cos, sin, kmem, vmem, mqn, ones_bd)


DIL_TILE_ROWS = 1024


def _dil_kernel(q_ref, kc_ref, kp_ref, vc_ref, vp_ref, o_ref, l_ref, *, dilation, spans):
    n = pl.program_id(2)
    blk = DIL_BLOCK
    span = dilation * blk
    i = lax.broadcasted_iota(jnp.int32, (blk, 2 * blk), 0)
    jj = lax.broadcasted_iota(jnp.int32, (blk, 2 * blk), 1)
    dist = blk + i - jj
    band = (dist >= 0) & (dist <= blk)
    band_first = band & ((n > 0) | (jj >= blk))
    head0 = lax.broadcasted_iota(jnp.int32, (1, PAIR), 1) < HEAD_DIM

    def sub(start):
        return pl.ds(start, blk, stride=dilation) if dilation > 1 else pl.ds(start, blk)

    for s in range(spans):
        for r in range(dilation):
            rows = sub(s * span + r)
            if s == 0:
                kprev, vprev, valid = kp_ref[0, sub(r), :], vp_ref[0, sub(r), :], band_first
            else:
                prow = sub((s - 1) * span + r)
                kprev, vprev, valid = kc_ref[0, prow, :], vc_ref[0, prow, :], band
            q = q_ref[0, rows, :]
            kw = jnp.concatenate([kprev, kc_ref[0, rows, :]], axis=0).astype(BF16)
            vw = jnp.concatenate([vprev, vc_ref[0, rows, :]], axis=0).astype(BF16)
            outs, lses = [], []
            for mask in (head0, jnp.logical_not(head0)):
                sc = _dg(jnp.where(mask, q, 0.0).astype(BF16), kw, NT)
                sc = jnp.where(valid, sc, NEG_INF)
                mx = jnp.max(sc, axis=-1, keepdims=True)
                p = jnp.exp(sc - mx)
                den = jnp.sum(p, axis=-1, keepdims=True)
                outs.append(_dg(p.astype(BF16), vw) * (1.0 / den))
                lses.append(mx + jnp.log(den))
            o_ref[0, rows, :] = jnp.where(head0, outs[0], outs[1])
            l_ref[0, rows, :] = jnp.where(head0, lses[0], lses[1])


def _dilated_group(q, k, v, gi, dilation):
    B, T, _ = q.shape
    span = dilation * DIL_BLOCK
    spans = max(1, DIL_TILE_ROWS // span)
    rows = spans * span
    pairs = GROUP_WIDTH // PAIR
    cur = pl.BlockSpec((1, rows, PAIR), lambda b, p, n: (b, n, gi * pairs + p))
    prev = pl.BlockSpec((1, span, PAIR), lambda b, p, n: (b, jnp.maximum(n * spans - 1, 0), gi * pairs + p))
    out = pl.BlockSpec((1, rows, PAIR), lambda b, p, n: (b, n, p))
    sd = jax.ShapeDtypeStruct((B, T, GROUP_WIDTH), F32)
    return pl.pallas_call(
        functools.partial(_dil_kernel, dilation=dilation, spans=spans),
        grid=(B, pairs, T // rows),
        in_specs=[cur, cur, prev, cur, prev],
        out_specs=[out, out],
        out_shape=[sd, sd],
        compiler_params=pltpu.CompilerParams(
            dimension_semantics=("parallel", "parallel", "parallel"), vmem_limit_bytes=VMEM_LIMIT),
        name=f"dilated_d{dilation}",
    )(q, k, k, v, v)


def _rope_tables(T):
    inv = ROPE_THETA ** (-jnp.arange(0, HEAD_DIM, 2, dtype=F32) / HEAD_DIM)
    ang = jnp.arange(T, dtype=F32)[:, None] * inv[None, :]
    cos, sin = jnp.cos(ang), jnp.sin(ang)
    cos_t = jnp.tile(cos, (1, LANES // (HEAD_DIM // 2)))
    sin_t = jnp.tile(jnp.concatenate([-sin, sin], axis=1), (1, LANES // HEAD_DIM))
    return cos_t, sin_t


def kernel(x, mem, attn_norm, a_w_in, a_mu, a_w0, a_w2, a_a0, a_a2, a_g2, a_k_k, a_k_a, a_r_k, a_lnx_w, a_lnx_b, a_w_out, kv_norm, kv_w, kv_k_norm, b_w_in, b_q_norm, b_w_out, mem_norm, mem_w_kv, mem_q_norm, mem_k_norm, ffn_norm, ffn_w_up, ffn_conv_w, ffn_conv_b, ffn_w_down):
    B, T, D = x.shape
    depth = attn_norm.shape[0]
    n_a = a_w_in.shape[0]
    seg = jnp.arange(MEM_WIDTH) // HEAD_DIM
    ones_bd = (seg[:, None] == seg[None, :]).astype(BF16)
    kmem, vmem = _mem_kv(mem, mem_norm, mem_w_kv, mem_k_norm, ones_bd)
    cos_t, sin_t = _rope_tables(T)
    k_sh = v_sh = None
    for i in range(depth):
        mqn = jnp.tile(mem_q_norm[i], MEM_HEADS).reshape(1, MEM_WIDTH)
        if i < n_a:
            j = i
            r, lw, k, v, kk, a, g, bon, ymem = _a_in(
                x, attn_norm[i], a_w_in[j], a_mu[j], a_w0[j], a_w2[j], a_a0[j], a_a2[j], a_g2[j],
                a_k_k[j], a_k_a[j], a_r_k[j].reshape(-1), kmem, vmem, i, mqn, ones_bd)
            ymix = _scan(r, lw, k, v, kk, a, g, bon, a_lnx_w[j], a_lnx_b[j])
            mix_kernel, mix_inputs, w_out, n_mix = _a_mix_ffn_kernel, [ymix, ymem], a_w_out[j], RWKV_WIDTH
        else:
            j = i - n_a
            q, k_new, v_new, ymem = _b_in(x, kv_norm, attn_norm[i], kv_w, b_w_in[j], kv_k_norm, b_q_norm[j],
                                          cos_t, sin_t, kmem, vmem, i, mqn, ones_bd)
            if j == 0:
                k_sh, v_sh = k_new, v_new
            outs, lses = [], []
            for gi, (_, dil) in enumerate(DIL_GROUPS):
                o, l = _dilated_group(q, k_sh, v_sh, gi, dil)
                outs.append(o)
                lses.append(l)
            mix_kernel, mix_inputs, w_out, n_mix = _b_mix_ffn_kernel, outs + lses + [ymem], b_w_out[j], GROUP_WIDTH
        x = _mix_ffn(mix_kernel, mix_inputs, x, w_out, n_mix,
                     ffn_norm[i], ffn_w_up[i], ffn_conv_w[i], ffn_conv_b[i], ffn_w_down[i])
    return x
```
